```python
import math
import jax, jax.numpy as jnp
from jax import lax
import numpy as np

D_MODEL = 1024
BATCH = 4
SEQ = 4096
DEPTH = 4

GRID_W = 64
CTX_LEN = 256
CHUNK = 128
N_MIXERS = 3
EPS = 1e-6
D_FF = -(-8 * D_MODEL // (3 * 256)) * 256

SSD_INNER = 2 * D_MODEL
SSD_HEAD_DIM = 64
SSD_HEADS = SSD_INNER // SSD_HEAD_DIM
SSD_GROUPS = 4
SSD_HPG = SSD_HEADS // SSD_GROUPS
SSD_STATE = 128
SSD_CONV = 5
SSD_CONV_DIM = SSD_INNER + 2 * SSD_GROUPS * SSD_STATE
SSD_IN_DIM = SSD_INNER + SSD_CONV_DIM + 2 * SSD_HEADS

GMLP_WIDTH = 2 * D_MODEL
GMLP_GROUPS = 8
GMLP_GROUP_DIM = GMLP_WIDTH // GMLP_GROUPS

RET_HEADS = D_MODEL // 256
RET_QK = D_MODEL // RET_HEADS
RET_V = 2 * D_MODEL // RET_HEADS
RET_IN_DIM = 2 * D_MODEL + 2 * (2 * D_MODEL)
ROPE_BASE = 10000.0

N_A = (DEPTH + 2) // 3
N_B = (DEPTH + 1) // 3
N_C = DEPTH // 3

kernel_name = "hybrid_ssd_gmlp_retention_dit"


def rmsnorm(x, g):
    xf = x.astype(jnp.float32)
    y = xf * lax.rsqrt(jnp.mean(xf * xf, axis=-1, keepdims=True) + EPS)
    return (y * g).astype(x.dtype)


def modulate(h, shift, scale):
    return h * (1.0 + scale) + shift


def swiglu(h, w1, w3, w2):
    return (jax.nn.silu(h @ w1) * (h @ w3)) @ w2


def dwconv(x, w, b):
    k = w.shape[0]
    y = lax.conv_general_dilated(x, w[:, None, :].astype(x.dtype), window_strides=(1,),
                                 padding=[(k // 2, k // 2)],
                                 dimension_numbers=("NWC", "WIO", "NWC"),
                                 feature_group_count=x.shape[-1])
    return y + b


def rope(t, pos):
    half = t.shape[-1] // 2
    inv = ROPE_BASE ** (-jnp.arange(half, dtype=jnp.float32) / half)
    ang = pos.astype(jnp.float32)[:, None] * inv
    cos = jnp.cos(ang)[:, None, :]
    sin = jnp.sin(ang)[:, None, :]
    t1 = t[..., :half].astype(jnp.float32)
    t2 = t[..., half:].astype(jnp.float32)
    return jnp.concatenate([t1 * cos - t2 * sin, t2 * cos + t1 * sin], axis=-1).astype(t.dtype)


def axial_rope(t, row, col):
    d = t.shape[-1] // 2
    return jnp.concatenate([rope(t[..., :d], row), rope(t[..., d:], col)], axis=-1)


def chunked_decay_scan(q, k, v, log_a, s0):
    bsz, length = q.shape[:2]
    nc = length // CHUNK

    def to_chunks(t):
        return jnp.swapaxes(t.reshape(bsz, nc, CHUNK, *t.shape[2:]), 0, 1)

    mask = jnp.tril(jnp.ones((CHUNK, CHUNK), bool))[None, :, :, None, None]

    def step(s, inp):
        qi, ki, vi, ai = inp
        cum = jnp.cumsum(ai.astype(jnp.float32), axis=1)
        seg = cum[:, :, None] - cum[:, None]
        decay = jnp.where(mask, jnp.exp(jnp.where(mask, seg, 0.0)), 0.0)
        scores = jnp.einsum("bign,bjgn->bijg", qi, ki)
        y = jnp.einsum("bijgr,bjgrp->bigrp", scores[..., None] * decay, vi)
        y = y + jnp.einsum("bign,bgrnp->bigrp", qi, s) * jnp.exp(cum)[..., None]
        tail = jnp.exp(cum[:, -1:] - cum)
        s = s * jnp.exp(cum[:, -1])[..., None, None] + jnp.einsum(
            "bjgn,bjgrp->bgrnp", ki, vi * tail[..., None])
        return s, y

    s, ys = lax.scan(step, s0, tuple(to_chunks(t) for t in (q, k, v, log_a)))
    y = jnp.swapaxes(ys, 0, 1).reshape(bsz, length, *v.shape[2:])
    return y, s


def bidir_prefix(ctx_in, lat_in, s0):
    qc, kc, vcf, acf, vcb, acb = ctx_in
    ql, kl, vlf, alf, vlb, alb = lat_in
    rev = lambda t: jnp.flip(t, axis=1)
    yc_f, sc_f = chunked_decay_scan(qc, kc, vcf, acf, s0)
    yc_b, sc_b = chunked_decay_scan(rev(qc), rev(kc), rev(vcb), rev(acb), s0)
    yl_f, _ = chunked_decay_scan(ql, kl, vlf, alf, sc_f)
    yl_b, _ = chunked_decay_scan(rev(ql), rev(kl), rev(vlb), rev(alb), sc_b)
    return yc_f + rev(yc_b), yl_f + rev(yl_b)


def ssd_mixer(a_ctx, a_lat, w_in, conv_w, conv_b, a_log_f, a_log_b, dt_bias_f, dt_bias_b,
              d_skip, norm_g, w_out, need_ctx):
    A_f = -jnp.exp(a_log_f.astype(jnp.float32)).reshape(SSD_GROUPS, SSD_HPG)
    A_b = -jnp.exp(a_log_b.astype(jnp.float32)).reshape(SSD_GROUPS, SSD_HPG)

    def project(h):
        bsz, length = h.shape[:2]
        z, xbc, dt = jnp.split(h @ w_in, [SSD_INNER, SSD_INNER + SSD_CONV_DIM], axis=-1)
        xbc = jax.nn.silu(dwconv(xbc, conv_w, conv_b))
        xs, bm, cm = jnp.split(xbc, [SSD_INNER, SSD_INNER + SSD_GROUPS * SSD_STATE], axis=-1)
        xs = xs.reshape(bsz, length, SSD_GROUPS, SSD_HPG, SSD_HEAD_DIM)
        bm = bm.reshape(bsz, length, SSD_GROUPS, SSD_STATE)
        cm = cm.reshape(bsz, length, SSD_GROUPS, SSD_STATE)
        dt_f = jax.nn.softplus(dt[..., :SSD_HEADS] + dt_bias_f).reshape(bsz, length, SSD_GROUPS, SSD_HPG)
        dt_b = jax.nn.softplus(dt[..., SSD_HEADS:] + dt_bias_b).reshape(bsz, length, SSD_GROUPS, SSD_HPG)
        scan_in = (cm, bm, xs * dt_f[..., None], dt_f * A_f, xs * dt_b[..., None], dt_b * A_b)
        return z, xs, scan_in

    def finish(y, z, xs):
        y = y + d_skip.reshape(SSD_GROUPS, SSD_HPG)[:, :, None] * xs
        y = y.reshape(*z.shape[:2], SSD_INNER) * jax.nn.silu(z)
        return rmsnorm(y, norm_g) @ w_out

    z_c, xs_c, in_c = project(a_ctx)
    z_l, xs_l, in_l = project(a_lat)
    s0 = jnp.zeros((a_lat.shape[0], SSD_GROUPS, SSD_HPG, SSD_STATE, SSD_HEAD_DIM), jnp.float32)
    y_c, y_l = bidir_prefix(in_c, in_l, s0)
    o_ctx = finish(y_c, z_c, xs_c) if need_ctx else None
    return o_ctx, finish(y_l, z_l, xs_l)


def gmlp_mixer(a_ctx, a_lat, w_in, norm_g, w_s, b_s, w_out, need_ctx):
    def mix(h):
        bsz, length = h.shape[:2]
        u, v = jnp.split(jax.nn.gelu(h @ w_in), 2, axis=-1)
        v = rmsnorm(v, norm_g).reshape(bsz, length // CHUNK, CHUNK, GMLP_GROUPS, GMLP_GROUP_DIM)
        v = jnp.einsum("gij,bcjgd->bcigd", w_s, v) + b_s.T[:, :, None]
        return (u * v.reshape(bsz, length, GMLP_WIDTH)) @ w_out

    o_ctx = mix(a_ctx) if need_ctx else None
    return o_ctx, mix(a_lat)


def retention_mixer(a_ctx, a_lat, w_in, decay_f, decay_b, w_out, row, col, need_ctx):
    lg_f = -jnp.exp(decay_f.astype(jnp.float32))
    lg_b = -jnp.exp(decay_b.astype(jnp.float32))

    def project(h, rotate):
        bsz, length = h.shape[:2]
        q, k, v, g = jnp.split(h @ w_in, [D_MODEL, 2 * D_MODEL, 4 * D_MODEL], axis=-1)
        q = q.reshape(bsz, length, RET_HEADS, RET_QK)
        k = k.reshape(bsz, length, RET_HEADS, RET_QK) * (RET_QK ** -0.5)
        if rotate:
            q = axial_rope(q, row, col)
            k = axial_rope(k, row, col)
        v = v.reshape(bsz, length, RET_HEADS, 1, RET_V)
        af = jnp.broadcast_to(lg_f[:, None], (bsz, length, RET_HEADS, 1))
        ab = jnp.broadcast_to(lg_b[:, None], (bsz, length, RET_HEADS, 1))
        return g, (q, k, v, af, v, ab)

    def finish(y, g):
        bsz, length = g.shape[:2]
        y = y.reshape(bsz, length, RET_HEADS, RET_V).astype(jnp.float32)
        y = y * lax.rsqrt(jnp.mean(y * y, axis=-1, keepdims=True) + EPS)
        y = y.reshape(bsz, length, 2 * D_MODEL).astype(g.dtype)
        return (jax.nn.silu(g) * y) @ w_out

    g_c, in_c = project(a_ctx, False)
    g_l, in_l = project(a_lat, True)
    s0 = jnp.zeros((a_lat.shape[0], RET_HEADS, 1, RET_QK, RET_V), jnp.float32)
    y_c, y_l = bidir_prefix(in_c, in_l, s0)
    o_ctx = finish(y_c, g_c) if need_ctx else None
    return o_ctx, finish(y_l, g_l)


def setup_inputs(seed: int = 0) -> dict:
    key = jax.random.key(seed)
    ks = iter(jax.random.split(key, 40))
    f32 = jnp.float32

    def nrm(shape, scale):
        return jax.random.normal(next(ks), shape, f32) * scale

    def gain(shape):
        return 1.0 + nrm(shape, 0.02)

    def dt_bias():
        dt = jnp.exp(jax.random.uniform(next(ks), (N_A, SSD_HEADS), f32, math.log(1e-3), math.log(1e-1)))
        return dt + jnp.log(-jnp.expm1(-dt))

    inv = D_MODEL ** -0.5
    x = nrm((BATCH, SEQ, D_MODEL), 1.0)
    c = nrm((BATCH, D_MODEL), 1.0)
    ctx = nrm((BATCH, CTX_LEN, D_MODEL), 1.0)
    c_ctx = nrm((D_MODEL,), 1.0)
    mod_w = nrm((DEPTH, D_MODEL, 6 * D_MODEL), 0.5 * inv)
    mod_b = nrm((DEPTH, 6 * D_MODEL), 0.02)
    norm1_g = gain((DEPTH, D_MODEL))
    norm2_g = gain((DEPTH, D_MODEL))
    ffn_w1 = nrm((DEPTH, D_MODEL, D_FF), inv)
    ffn_w3 = nrm((DEPTH, D_MODEL, D_FF), inv)
    ffn_w2 = nrm((DEPTH, D_FF, D_MODEL), D_FF ** -0.5)
    ssd_w_in = nrm((N_A, D_MODEL, SSD_IN_DIM), inv)
    ssd_conv_w = nrm((N_A, SSD_CONV, SSD_CONV_DIM), SSD_CONV ** -0.5)
    ssd_conv_b = nrm((N_A, SSD_CONV_DIM), 0.02)
    ssd_a_log_f = jnp.log(jax.random.uniform(next(ks), (N_A, SSD_HEADS), f32, 1.0, 16.0))
    ssd_a_log_b = jnp.log(jax.random.uniform(next(ks), (N_A, SSD_HEADS), f32, 1.0, 16.0))
    ssd_dt_bias_f = dt_bias()
    ssd_dt_bias_b = dt_bias()
    ssd_d = gain((N_A, SSD_HEADS))
    ssd_norm_g = gain((N_A, SSD_INNER))
    ssd_w_out = nrm((N_A, SSD_INNER, D_MODEL), SSD_INNER ** -0.5)
    gmlp_w_in = nrm((N_B, D_MODEL, 2 * GMLP_WIDTH), inv)
    gmlp_norm_g = gain((N_B, GMLP_WIDTH))
    gmlp_w_s = nrm((N_B, GMLP_GROUPS, CHUNK, CHUNK), CHUNK ** -0.5)
    gmlp_b_s = gain((N_B, GMLP_GROUPS, CHUNK))
    gmlp_w_out = nrm((N_B, GMLP_WIDTH, D_MODEL), GMLP_WIDTH ** -0.5)
    ret_w_in = nrm((N_C, D_MODEL, RET_IN_DIM), inv)
    base = jnp.log(-jnp.log1p(-(2.0 ** (-5.0 - jnp.arange(RET_HEADS, dtype=f32)))))
    ret_decay_f = base + nrm((N_C, RET_HEADS), 0.05)
    ret_decay_b = base + nrm((N_C, RET_HEADS), 0.05)
    ret_w_out = nrm((N_C, 2 * D_MODEL, D_MODEL), (2 * D_MODEL) ** -0.5)
    final_g = gain((D_MODEL,))
    return {"x": x, "c": c, "ctx": ctx, "c_ctx": c_ctx, "mod_w": mod_w, "mod_b": mod_b,
            "norm1_g": norm1_g, "norm2_g": norm2_g, "ffn_w1": ffn_w1, "ffn_w3": ffn_w3, "ffn_w2": ffn_w2,
            "ssd_w_in": ssd_w_in, "ssd_conv_w": ssd_conv_w, "ssd_conv_b": ssd_conv_b,
            "ssd_a_log_f": ssd_a_log_f, "ssd_a_log_b": ssd_a_log_b,
            "ssd_dt_bias_f": ssd_dt_bias_f, "ssd_dt_bias_b": ssd_dt_bias_b, "ssd_d": ssd_d,
            "ssd_norm_g": ssd_norm_g, "ssd_w_out": ssd_w_out,
            "gmlp_w_in": gmlp_w_in, "gmlp_norm_g": gmlp_norm_g, "gmlp_w_s": gmlp_w_s,
            "gmlp_b_s": gmlp_b_s, "gmlp_w_out": gmlp_w_out,
            "ret_w_in": ret_w_in, "ret_decay_f": ret_decay_f, "ret_decay_b": ret_decay_b,
            "ret_w_out": ret_w_out, "final_g": final_g}


def reference(x, c, ctx, c_ctx, mod_w, mod_b, norm1_g, norm2_g, ffn_w1, ffn_w3, ffn_w2,
              ssd_w_in, ssd_conv_w, ssd_conv_b, ssd_a_log_f, ssd_a_log_b, ssd_dt_bias_f,
              ssd_dt_bias_b, ssd_d, ssd_norm_g, ssd_w_out,
              gmlp_w_in, gmlp_norm_g, gmlp_w_s, gmlp_b_s, gmlp_w_out,
              ret_w_in, ret_decay_f, ret_decay_b, ret_w_out, final_g):
    n_tok = x.shape[1]
    ROWS = n_tok // GRID_W
    pos = jnp.arange(ROWS * GRID_W)
    row = pos // GRID_W
    col = pos % GRID_W
    c_act = jax.nn.silu(c)
    c_ctx_act = jax.nn.silu(c_ctx)
    h_ctx = ctx
    for i in range(DEPTH):
        last = i == DEPTH - 1
        kind, j = i % N_MIXERS, i // N_MIXERS
        m_lat = (c_act @ mod_w[i] + mod_b[i])[:, None, :]
        m_ctx = c_ctx_act @ mod_w[i] + mod_b[i]
        sh1, sc1, g1, sh2, sc2, g2 = jnp.split(m_lat, 6, axis=-1)
        csh1, csc1, cg1, csh2, csc2, cg2 = jnp.split(m_ctx, 6, axis=-1)
        a_lat = modulate(rmsnorm(x, norm1_g[i]), sh1, sc1)
        a_ctx = modulate(rmsnorm(h_ctx, norm1_g[i]), csh1, csc1)
        if kind == 0:
            o_ctx, o_lat = ssd_mixer(a_ctx, a_lat, ssd_w_in[j], ssd_conv_w[j], ssd_conv_b[j],
                                     ssd_a_log_f[j], ssd_a_log_b[j], ssd_dt_bias_f[j],
                                     ssd_dt_bias_b[j], ssd_d[j], ssd_norm_g[j], ssd_w_out[j],
                                     not last)
        elif kind == 1:
            o_ctx, o_lat = gmlp_mixer(a_ctx, a_lat, gmlp_w_in[j], gmlp_norm_g[j], gmlp_w_s[j],
                                      gmlp_b_s[j], gmlp_w_out[j], not last)
        else:
            o_ctx, o_lat = retention_mixer(a_ctx, a_lat, ret_w_in[j], ret_decay_f[j],
                                           ret_decay_b[j], ret_w_out[j], row, col, not last)
        x = x + g1 * o_lat
        x = x + g2 * swiglu(modulate(rmsnorm(x, norm2_g[i]), sh2, sc2), ffn_w1[i], ffn_w3[i], ffn_w2[i])
        if not last:
            h_ctx = h_ctx + cg1 * o_ctx
            h_ctx = h_ctx + cg2 * swiglu(modulate(rmsnorm(h_ctx, norm2_g[i]), csh2, csc2),
                                         ffn_w1[i], ffn_w3[i], ffn_w2[i])
    return rmsnorm(x, final_g)
```

```python
import functools

import jax
import jax.numpy as jnp
from jax import lax
from jax.experimental import pallas as pl
from jax.experimental.pallas import tpu as pltpu

F32 = jnp.float32
BF16 = jnp.bfloat16
HI = lax.Precision.HIGHEST

EPS = 1e-6
CHUNK = 128
GRID_W = 64
ROPE_BASE = 10000.0
N_MIXERS = 3
SSD_HEAD_DIM = 64
SSD_GROUPS = 4
SSD_STATE = 128
SSD_CONV = 5
GMLP_GROUPS = 8
RET_HEADS = 4
LANES = 128
BF16_ROWS = 16
FF_CHUNK = 256
PROJ_CHUNK = 256
VMEM_LIMIT = 56 * 1024 * 1024
NEG_BIG = -1e30


def _cparams(*sem):
    return pltpu.CompilerParams(dimension_semantics=sem, vmem_limit_bytes=VMEM_LIMIT)


def _pick_tile(cands, *dims):
    for t in cands:
        if all(d % t == 0 for d in dims):
            return t
    raise ValueError(f"no tile in {cands} divides {dims}")


def _silu(x):
    return x / (1.0 + jnp.exp(-x))


def _gelu_tanh(x):
    return 0.5 * x * (1.0 + jnp.tanh(0.7978845608028654 * (x + 0.044715 * (x * x * x))))


def _softplus(x):
    return jnp.maximum(x, 0.0) + jnp.log(1.0 + jnp.exp(-jnp.abs(x)))


def _norm_mod(x, g, sh, sc):
    ms = jnp.mean(x * x, axis=-1, keepdims=True)
    y = x * lax.rsqrt(ms + EPS) * g
    return y * (1.0 + sc) + sh


def _const_spec(shape):
    nd = len(shape)
    return pl.BlockSpec(shape, lambda *_: (0,) * nd, pipeline_mode=pl.Buffered(1))


def _mod_kernel(c_ref, w_ref, b_ref, o_ref):
    a = _silu(c_ref[...])
    o_ref[...] = jnp.dot(a, w_ref[...], preferred_element_type=F32, precision=HI) + b_ref[...]


def _modulation(cvec, mod_w, mod_b):
    depth, d, d6 = mod_w.shape
    r = cvec.shape[0]
    nk = d6 // d
    return pl.pallas_call(
        _mod_kernel,
        grid=(depth, nk),
        in_specs=[
            pl.BlockSpec((r, d), lambda l, k: (0, 0)),
            pl.BlockSpec((None, d, d), lambda l, k: (l, 0, k)),
            pl.BlockSpec((None, 1, d), lambda l, k: (l, 0, k)),
        ],
        out_specs=pl.BlockSpec((None, r, d), lambda l, k: (l, 0, k)),
        out_shape=jax.ShapeDtypeStruct((depth, r, d6), F32),
        compiler_params=_cparams("arbitrary", "arbitrary"),
        name="modulation",
    )(cvec, mod_w, mod_b.reshape(depth, 1, d6))


class _Layout:
    def __init__(self, nb, seq, ctx):
        self.nb, self.seq, self.ctx = nb, seq, ctx
        self.t_lat = nb * seq
        self.t_ctx = nb * ctx
        self.t_all = self.t_lat + self.t_ctx
        self.ncl = seq // CHUNK
        self.ncc = ctx // CHUNK

    def mod_row(self, i, tm):
        return jnp.minimum((i * tm) // self.seq, self.nb)

    def chunk_block(self, b, s, reverse):
        if reverse:
            c_ctx, c_lat = self.ncc - 1 - s, self.ncl - 1 - (s - self.ncc)
        else:
            c_ctx, c_lat = s, s - self.ncc
        return jnp.where(s < self.ncc, self.nb * self.ncl + b * self.ncc + c_ctx, b * self.ncl + c_lat)


def _mod_spec(lay, tm, k, d):
    return pl.BlockSpec((None, 1, d), lambda i, *_: (lay.mod_row(i, tm) * 6 + k, 0, 0))


def _ssd_proj_kernel(x_ref, g_ref, sh_ref, sc_ref, w_ref, wdt_ref, dtb_ref, o_ref, dt_ref, *, n_chunks):
    hb = _norm_mod(x_ref[...], g_ref[...], sh_ref[...], sc_ref[...]).astype(BF16)
    for j in range(n_chunks):
        cs = slice(j * PROJ_CHUNK, (j + 1) * PROJ_CHUNK)
        o_ref[:, cs] = jnp.dot(hb, w_ref[:, cs], preferred_element_type=F32).astype(BF16)
    dr = jnp.dot(hb, wdt_ref[...], preferred_element_type=F32) + dtb_ref[...]
    dt_ref[...] = _softplus(dr)


def _gmlp_proj_kernel(x_ref, g_ref, sh_ref, sc_ref, w_ref, o_ref, *, n_chunks):
    hb = _norm_mod(x_ref[...], g_ref[...], sh_ref[...], sc_ref[...]).astype(BF16)
    for j in range(n_chunks):
        cs = slice(j * PROJ_CHUNK, (j + 1) * PROJ_CHUNK)
        acc = jnp.dot(hb, w_ref[:, cs], preferred_element_type=F32)
        o_ref[:, cs] = _gelu_tanh(acc).astype(BF16)


def _ret_proj_kernel(x_ref, g_ref, sh_ref, sc_ref, w_ref, cos_ref, sin_ref, o_ref, *, n_chunks, n_heads, k_scale):
    hb = _norm_mod(x_ref[...], g_ref[...], sh_ref[...], sc_ref[...]).astype(BF16)
    half = LANES // 2
    for j in range(n_chunks):
        cs = slice(j * PROJ_CHUNK, (j + 1) * PROJ_CHUNK)
        acc = jnp.dot(hb, w_ref[:, cs], preferred_element_type=F32)
        if j < 2 * n_heads:
            rolled = jnp.concatenate(
                [pltpu.roll(acc[:, :LANES], half, 1), pltpu.roll(acc[:, LANES:], half, 1)], axis=1)
            acc = acc * cos_ref[...] + rolled * sin_ref[...]
            if j >= n_heads:
                acc = acc * k_scale
        o_ref[:, cs] = acc.astype(BF16)


def _proj_common_specs(lay, tm, d, n_out):
    return [
        pl.BlockSpec((tm, d), lambda i: (i, 0)),
        _const_spec((1, d)),
        _mod_spec(lay, tm, 0, d),
        _mod_spec(lay, tm, 1, d),
        _const_spec((d, n_out)),
    ]


def _ssd_proj(lay, xall, g, modl, w_main, w_dt, dt_bias):
    t, d = xall.shape
    n_out = w_main.shape[1]
    tm = _pick_tile((512, 256, 128), lay.seq, lay.t_ctx)
    return pl.pallas_call(
        functools.partial(_ssd_proj_kernel, n_chunks=n_out // PROJ_CHUNK),
        grid=(t // tm,),
        in_specs=_proj_common_specs(lay, tm, d, n_out) + [_const_spec((d, LANES)), _const_spec((1, LANES))],
        out_specs=[pl.BlockSpec((tm, n_out), lambda i: (i, 0)), pl.BlockSpec((tm, LANES), lambda i: (i, 0))],
        out_shape=[jax.ShapeDtypeStruct((t, n_out), BF16), jax.ShapeDtypeStruct((t, LANES), F32)],
        compiler_params=_cparams("arbitrary"),
        name="ssd_proj",
    )(xall, g, modl, modl, w_main, w_dt, dt_bias)


def _gmlp_proj(lay, xall, g, modl, w_in):
    t, d = xall.shape
    n_out = w_in.shape[1]
    tm = _pick_tile((512, 256, 128), lay.seq, lay.t_ctx)
    return pl.pallas_call(
        functools.partial(_gmlp_proj_kernel, n_chunks=n_out // PROJ_CHUNK),
        grid=(t // tm,),
        in_specs=_proj_common_specs(lay, tm, d, n_out),
        out_specs=pl.BlockSpec((tm, n_out), lambda i: (i, 0)),
        out_shape=jax.ShapeDtypeStruct((t, n_out), BF16),
        compiler_params=_cparams("arbitrary"),
        name="gmlp_proj",
    )(xall, g, modl, modl, w_in)


def _rope_tables(lay, tm, head_dim):
    quarter = head_dim // 4
    pos = jnp.arange(lay.seq)
    inv = ROPE_BASE ** (-jnp.arange(quarter, dtype=F32) / quarter)
    ang_r = (pos // GRID_W).astype(F32)[:, None] * inv
    ang_c = (pos % GRID_W).astype(F32)[:, None] * inv
    cos = jnp.concatenate([jnp.cos(ang_r)] * 2 + [jnp.cos(ang_c)] * 2, axis=1)
    sin = jnp.concatenate([-jnp.sin(ang_r), jnp.sin(ang_r), -jnp.sin(ang_c), jnp.sin(ang_c)], axis=1)
    cos = jnp.concatenate([cos, jnp.ones((tm, head_dim), F32)], axis=0)
    sin = jnp.concatenate([sin, jnp.zeros((tm, head_dim), F32)], axis=0)
    return cos, sin


def _ret_proj(lay, xall, g, modl, w_in):
    t, d = xall.shape
    n_out = w_in.shape[1]
    head_dim = d // RET_HEADS
    assert head_dim == PROJ_CHUNK
    tm = _pick_tile((512, 256, 128), lay.seq, lay.t_ctx)
    cos, sin = _rope_tables(lay, tm, head_dim)
    lat_tiles = lay.seq // tm

    def tab_idx(i):
        return (jnp.where(i * tm < lay.t_lat, i % lat_tiles, lat_tiles), 0)

    return pl.pallas_call(
        functools.partial(_ret_proj_kernel, n_chunks=n_out // PROJ_CHUNK, n_heads=RET_HEADS,
                          k_scale=head_dim ** -0.5),
        grid=(t // tm,),
        in_specs=_proj_common_specs(lay, tm, d, n_out) + [
            pl.BlockSpec((tm, head_dim), tab_idx), pl.BlockSpec((tm, head_dim), tab_idx)],
        out_specs=pl.BlockSpec((tm, n_out), lambda i: (i, 0)),
        out_shape=jax.ShapeDtypeStruct((t, n_out), BF16),
        compiler_params=_cparams("arbitrary"),
        name="ret_proj",
    )(xall, g, modl, modl, w_in, cos, sin)


def _conv_kernel(main_ref, prev_ref, next_ref, w_ref, b_ref, o_ref, ext_ref, *, tm, lay, k_taps):
    i = pl.program_id(0)
    t0 = i * tm
    in_lat = t0 < lay.t_lat
    off = jnp.where(in_lat, t0, t0 - lay.t_lat)
    length = jnp.where(in_lat, lay.seq, lay.ctx)
    at_start = (off % length) == 0
    at_end = ((off + tm) % length) == 0
    halo = BF16_ROWS
    prev = prev_ref[...].astype(F32)
    nxt = next_ref[...].astype(F32)
    ext_ref[0:halo, :] = jnp.where(at_start, 0.0, prev)
    ext_ref[halo:halo + tm, :] = main_ref[...].astype(F32)
    ext_ref[halo + tm:, :] = jnp.where(at_end, 0.0, nxt)
    acc = b_ref[...] + w_ref[0:1, :] * ext_ref[pl.ds(halo - k_taps // 2, tm), :]
    for k in range(1, k_taps):
        acc = acc + w_ref[k:k + 1, :] * ext_ref[pl.ds(halo - k_taps // 2 + k, tm), :]
    o_ref[...] = _silu(acc).astype(BF16)


def _ssd_conv(lay, zx, conv_w, conv_b, col0):
    t = zx.shape[0]
    k_taps, c = conv_w.shape
    tm = _pick_tile((256, 128), lay.seq, lay.ctx)
    cw = _pick_tile((c, 1024, 512, 256, 128), c, col0)
    ncb = c // cw
    cb0 = col0 // cw
    hb = tm // BF16_ROWS
    last_hb = t // BF16_ROWS - 1
    return pl.pallas_call(
        functools.partial(_conv_kernel, tm=tm, lay=lay, k_taps=k_taps),
        grid=(t // tm, ncb),
        in_specs=[
            pl.BlockSpec((tm, cw), lambda i, j: (i, cb0 + j)),
            pl.BlockSpec((BF16_ROWS, cw), lambda i, j: (jnp.maximum(i * hb - 1, 0), cb0 + j)),
            pl.BlockSpec((BF16_ROWS, cw), lambda i, j: (jnp.minimum((i + 1) * hb, last_hb), cb0 + j)),
            pl.BlockSpec((k_taps, cw), lambda i, j: (0, j)),
            pl.BlockSpec((1, cw), lambda i, j: (0, j)),
        ],
        out_specs=pl.BlockSpec((tm, cw), lambda i, j: (i, j)),
        out_shape=jax.ShapeDtypeStruct((t, c), BF16),
        scratch_shapes=[pltpu.VMEM((tm + 2 * BF16_ROWS, cw), F32)],
        compiler_params=_cparams("arbitrary", "arbitrary"),
        name="ssd_conv",
    )(zx, zx, zx, conv_w, conv_b.reshape(1, c))


def _tri_masks(q):
    ri = lax.broadcasted_iota(jnp.int32, (q, q), 0)
    ci = lax.broadcasted_iota(jnp.int32, (q, q), 1)
    return ri >= ci, ri <= ci


def _ssd_cums(dt, alog_ref):
    q = dt.shape[0]
    tril, triu = _tri_masks(q)
    a = dt * (-jnp.exp(alog_ref[...]))
    cum_f = jnp.dot(tril.astype(F32), a, preferred_element_type=F32, precision=HI)
    cum_b = jnp.dot(triu.astype(F32), a, preferred_element_type=F32, precision=HI)
    return cum_f, cum_b, tril, triu


def _ssd_bwd_kernel(x_ref, b_ref, c_ref, dt_ref, alog_ref, expb_ref, yb_ref, s_ref, *, groups):
    @pl.when(pl.program_id(1) == 0)
    def _():
        s_ref[...] = jnp.zeros_like(s_ref)

    q = dt_ref.shape[0]
    gw = x_ref.shape[1] // groups
    n = b_ref.shape[1] // groups
    dt = dt_ref[...]
    _, cum_b, _, _ = _ssd_cums(dt, alog_ref)
    expb = expb_ref[...]
    e_in = jnp.dot(jnp.exp(cum_b).astype(BF16), expb, preferred_element_type=F32)
    w_in = jnp.dot((dt * jnp.exp(cum_b[0:1, :] - cum_b)).astype(BF16), expb, preferred_element_type=F32)
    for g in range(groups):
        gs = slice(g * gw, (g + 1) * gw)
        ns = slice(g * n, (g + 1) * n)
        s_g = s_ref[:, gs]
        y = jnp.dot(c_ref[:, ns], s_g.astype(BF16), preferred_element_type=F32) * e_in[:, gs]
        yb_ref[:, gs] = y.astype(BF16)
        xw = (x_ref[:, gs].astype(F32) * w_in[:, gs]).astype(BF16)
        upd = lax.dot_general(b_ref[:, ns], xw, (((0,), (0,)), ((), ())), preferred_element_type=F32)
        s_ref[:, gs] = s_g * e_in[0:1, gs] + upd


def _ssd_fwd_kernel(x_ref, b_ref, c_ref, dt_ref, z_ref, yb_ref, alog_ref, expf_ref, dskip_ref, ng_ref,
                    o_ref, s_ref, y_ref, *, groups, hpg, p):
    @pl.when(pl.program_id(1) == 0)
    def _():
        s_ref[...] = jnp.zeros_like(s_ref)

    q = dt_ref.shape[0]
    inner = x_ref.shape[1]
    gw = inner // groups
    n = b_ref.shape[1] // groups
    heads = groups * hpg
    dt = dt_ref[...]
    cum_f, cum_b, tril, triu = _ssd_cums(dt, alog_ref)
    cum_ft, cum_bt, dt_t = cum_f.T, cum_b.T, dt.T
    expf = expf_ref[...]
    e_f = jnp.dot(jnp.exp(cum_f).astype(BF16), expf, preferred_element_type=F32)
    w_f = jnp.dot((dt * jnp.exp(cum_f[q - 1:q, :] - cum_f)).astype(BF16), expf, preferred_element_type=F32)
    lane = lax.broadcasted_iota(jnp.int32, (q, 2 * p), 1)
    left = lane < p
    ssq = jnp.zeros((q, 1), F32)
    for g in range(groups):
        gs = slice(g * gw, (g + 1) * gw)
        ns = slice(g * n, (g + 1) * n)
        c_g = c_ref[:, ns]
        b_g = b_ref[:, ns]
        scores = lax.dot_general(c_g, b_g, (((1,), (1,)), ((), ())), preferred_element_type=F32)
        s_g = s_ref[:, gs]
        pieces = []
        for pr in range(hpg // 2):
            ms = []
            for r in (g * hpg + 2 * pr, g * hpg + 2 * pr + 1):
                rb = heads + r
                seg_f = jnp.where(tril, cum_f[:, r:r + 1] - cum_ft[r:r + 1, :], NEG_BIG)
                seg_b = jnp.where(triu, cum_b[:, rb:rb + 1] - cum_bt[rb:rb + 1, :], NEG_BIG)
                dec = jnp.exp(seg_f) * dt_t[r:r + 1, :] + jnp.exp(seg_b) * dt_t[rb:rb + 1, :]
                ms.append((scores * dec).astype(BF16))
            lhs = jnp.concatenate(ms, axis=1)
            ps = slice(g * gw + pr * 2 * p, g * gw + (pr + 1) * 2 * p)
            xp = x_ref[:, ps]
            zero = jnp.zeros_like(xp)
            rhs = jnp.concatenate([jnp.where(left, xp, zero), jnp.where(left, zero, xp)], axis=0)
            pieces.append(jnp.dot(lhs, rhs, preferred_element_type=F32))
        x_g = x_ref[:, gs].astype(F32)
        y = jnp.concatenate(pieces, axis=1)
        y = y + jnp.dot(c_g, s_g.astype(BF16), preferred_element_type=F32) * e_f[:, gs]
        y = y + yb_ref[:, gs].astype(F32) + dskip_ref[:, gs] * x_g
        y = y * _silu(z_ref[:, gs].astype(F32))
        ssq = ssq + jnp.sum(y * y, axis=-1, keepdims=True)
        y_ref[:, gs] = y
        xw = (x_g * w_f[:, gs]).astype(BF16)
        upd = lax.dot_general(b_g, xw, (((0,), (0,)), ((), ())), preferred_element_type=F32)
        s_ref[:, gs] = s_g * e_f[q - 1:q, gs] + upd
    rms = lax.rsqrt(ssq / inner + EPS)
    o_ref[...] = (y_ref[...] * rms * ng_ref[...]).astype(BF16)


def _ssd_scan(lay, zx, xbc, dt, alog, d_skip, norm_g):
    t = zx.shape[0]
    inner = d_skip.shape[0] * SSD_HEAD_DIM
    heads = d_skip.shape[0]
    hpg = heads // SSD_GROUPS
    gn = SSD_GROUPS * SSD_STATE
    assert 2 * heads <= LANES and inner % gn == 0 and hpg % 2 == 0
    nsteps = lay.ncc + lay.ncl
    xb, bb, cb = 0, inner // gn, inner // gn + 1

    lane_head = jnp.arange(LANES)[:, None]
    col_head = (jnp.arange(inner) // SSD_HEAD_DIM)[None, :]
    exp_f = (lane_head == col_head).astype(BF16)
    exp_b = (lane_head == col_head + heads).astype(BF16)
    dskip = jnp.repeat(d_skip, SSD_HEAD_DIM).reshape(1, inner)

    def specs(reverse):
        cm = lambda b, s: lay.chunk_block(b, s, reverse)
        return dict(
            x=pl.BlockSpec((CHUNK, inner), lambda b, s: (cm(b, s), xb)),
            b=pl.BlockSpec((CHUNK, gn), lambda b, s: (cm(b, s), bb)),
            c=pl.BlockSpec((CHUNK, gn), lambda b, s: (cm(b, s), cb)),
            dt=pl.BlockSpec((CHUNK, LANES), lambda b, s: (cm(b, s), 0)),
            row=pl.BlockSpec((CHUNK, inner), lambda b, s: (cm(b, s), 0)),
        )

    sb = specs(True)
    yb = pl.pallas_call(
        functools.partial(_ssd_bwd_kernel, groups=SSD_GROUPS),
        grid=(lay.nb, nsteps),
        in_specs=[sb["x"], sb["b"], sb["c"], sb["dt"], _const_spec((1, LANES)), _const_spec((LANES, inner))],
        out_specs=sb["row"],
        out_shape=jax.ShapeDtypeStruct((t, inner), BF16),
        scratch_shapes=[pltpu.VMEM((SSD_STATE, inner), F32)],
        compiler_params=_cparams("arbitrary", "arbitrary"),
        name="ssd_scan_bwd",
    )(xbc, xbc, xbc, dt, alog, exp_b)

    sf = specs(False)
    return pl.pallas_call(
        functools.partial(_ssd_fwd_kernel, groups=SSD_GROUPS, hpg=hpg, p=SSD_HEAD_DIM),
        grid=(lay.nb, nsteps),
        in_specs=[sf["x"], sf["b"], sf["c"], sf["dt"], sf["row"], sf["row"], _const_spec((1, LANES)),
                  _const_spec((LANES, inner)), _const_spec((1, inner)), _const_spec((1, inner))],
        out_specs=sf["row"],
        out_shape=jax.ShapeDtypeStruct((t, inner), BF16),
        scratch_shapes=[pltpu.VMEM((SSD_STATE, inner), F32), pltpu.VMEM((CHUNK, inner), F32)],
        compiler_params=_cparams("arbitrary", "arbitrary"),
        name="ssd_scan_fwd",
    )(xbc, xbc, xbc, dt, zx, yb, alog, exp_f, dskip, norm_g.reshape(1, inner))


def _gmlp_gate_kernel(u_ref, v_ref, ng_ref, ws_ref, bias_ref, o_ref, *, groups):
    tm, width = u_ref.shape
    gd = width // groups
    v = v_ref[...].astype(F32)
    ms = jnp.mean(v * v, axis=-1, keepdims=True)
    vn = (v * lax.rsqrt(ms + EPS) * ng_ref[...]).astype(BF16)
    for c in range(tm // CHUNK):
        rs = slice(c * CHUNK, (c + 1) * CHUNK)
        for g in range(groups):
            gs = slice(g * gd, (g + 1) * gd)
            mixed = jnp.dot(ws_ref[g], vn[rs, gs], preferred_element_type=F32) + bias_ref[:, gs]
            o_ref[rs, gs] = (u_ref[rs, gs].astype(F32) * mixed).astype(BF16)


def _gmlp_gate(lay, uv, norm_g, w_s, b_s):
    t = uv.shape[0]
    width = uv.shape[1] // 2
    groups = w_s.shape[0]
    tm = _pick_tile((512, 256, 128), lay.seq, lay.t_ctx)
    bias = jnp.repeat(b_s.T, width // groups, axis=1)
    return pl.pallas_call(
        functools.partial(_gmlp_gate_kernel, groups=groups),
        grid=(t // tm,),
        in_specs=[
            pl.BlockSpec((tm, width), lambda i: (i, 0)),
            pl.BlockSpec((tm, width), lambda i: (i, 1)),
            _const_spec((1, width)),
            _const_spec((groups, CHUNK, CHUNK)),
            _const_spec((CHUNK, width)),
        ],
        out_specs=pl.BlockSpec((tm, width), lambda i: (i, 0)),
        out_shape=jax.ShapeDtypeStruct((t, width), BF16),
        compiler_params=_cparams("arbitrary"),
        name="gmlp_gate",
    )(uv, uv, norm_g.reshape(1, width), w_s.astype(BF16), bias)


def _ret_decay_tiles(dec_ref, row, h, q):
    return -jnp.exp(jnp.full((q, q), dec_ref[row, h], F32))


def _ret_bwd_kernel(dec_ref, q_ref, k_ref, v_ref, yb_ref, s_ref, *, heads):
    @pl.when(pl.program_id(1) == 0)
    def _():
        s_ref[...] = jnp.zeros_like(s_ref)

    q = q_ref.shape[0]
    dk = q_ref.shape[1] // heads
    dv = v_ref.shape[1] // heads
    pos = lax.broadcasted_iota(jnp.int32, (q, q), 0).astype(F32)
    for h in range(heads):
        lg = _ret_decay_tiles(dec_ref, 1, h, q)
        e_in = jnp.exp(lg * (q - pos))
        tail = jnp.exp(lg * pos)
        s_h = s_ref[h]
        ks = slice(h * dk, (h + 1) * dk)
        vs = slice(h * dv, (h + 1) * dv)
        y = jnp.dot(q_ref[:, ks], s_h.astype(BF16), preferred_element_type=F32)
        yb_ref[:, vs] = (y * jnp.concatenate([e_in] * (dv // q), axis=1)).astype(BF16)
        kt = (k_ref[:, ks].astype(F32) * jnp.concatenate([tail] * (dk // q), axis=1)).astype(BF16)
        upd = lax.dot_general(kt, v_ref[:, vs], (((0,), (0,)), ((), ())), preferred_element_type=F32)
        s_ref[h] = s_h * jnp.concatenate([jnp.exp(lg[0:1, :] * q)] * (dv // q), axis=1) + upd


def _ret_fwd_kernel(dec_ref, q_ref, k_ref, v_ref, g_ref, yb_ref, o_ref, s_ref, *, heads):
    @pl.when(pl.program_id(1) == 0)
    def _():
        s_ref[...] = jnp.zeros_like(s_ref)

    q = q_ref.shape[0]
    dk = q_ref.shape[1] // heads
    dv = v_ref.shape[1] // heads
    pos = lax.broadcasted_iota(jnp.int32, (q, q), 0).astype(F32)
    diff = pos - lax.broadcasted_iota(jnp.int32, (q, q), 1).astype(F32)
    for h in range(heads):
        lg_f = _ret_decay_tiles(dec_ref, 0, h, q)
        lg_b = _ret_decay_tiles(dec_ref, 1, h, q)
        dec = (jnp.where(diff >= 0, jnp.exp(lg_f * jnp.maximum(diff, 0.0)), 0.0)
               + jnp.where(diff <= 0, jnp.exp(lg_b * jnp.maximum(-diff, 0.0)), 0.0))
        e_in = jnp.exp(lg_f * (pos + 1.0))
        tail = jnp.exp(lg_f * (q - 1.0 - pos))
        s_h = s_ref[h]
        ks = slice(h * dk, (h + 1) * dk)
        vs = slice(h * dv, (h + 1) * dv)
        q_h, k_h, v_h = q_ref[:, ks], k_ref[:, ks], v_ref[:, vs]
        scores = lax.dot_general(q_h, k_h, (((1,), (1,)), ((), ())), preferred_element_type=F32)
        y = jnp.dot((scores * dec).astype(BF16), v_h, preferred_element_type=F32)
        y = y + jnp.dot(q_h, s_h.astype(BF16), preferred_element_type=F32) * jnp.concatenate(
            [e_in] * (dv // q), axis=1)
        y = y + yb_ref[:, vs].astype(F32)
        y = y * lax.rsqrt(jnp.mean(y * y, axis=-1, keepdims=True) + EPS)
        o_ref[:, vs] = (_silu(g_ref[:, vs].astype(F32)) * y).astype(BF16)
        kt = (k_h.astype(F32) * jnp.concatenate([tail] * (dk // q), axis=1)).astype(BF16)
        upd = lax.dot_general(kt, v_h, (((0,), (0,)), ((), ())), preferred_element_type=F32)
        s_ref[h] = s_h * jnp.concatenate([jnp.exp(lg_f[0:1, :] * q)] * (dv // q), axis=1) + upd


def _ret_scan(lay, qkvg, decay_f, decay_b, d):
    t = qkvg.shape[0]
    heads = decay_f.shape[0]
    dv = 2 * d
    nsteps = lay.ncc + lay.ncl
    dec = jnp.stack([decay_f, decay_b]).astype(F32)
    smem = pl.BlockSpec(memory_space=pltpu.SMEM)
    state = pltpu.VMEM((heads, d // heads, dv // heads), F32)

    def specs(reverse):
        cm = lambda b, s: lay.chunk_block(b, s, reverse)
        return dict(
            q=pl.BlockSpec((CHUNK, d), lambda b, s: (cm(b, s), 0)),
            k=pl.BlockSpec((CHUNK, d), lambda b, s: (cm(b, s), 1)),
            v=pl.BlockSpec((CHUNK, dv), lambda b, s: (cm(b, s), 1)),
            g=pl.BlockSpec((CHUNK, dv), lambda b, s: (cm(b, s), 2)),
            row=pl.BlockSpec((CHUNK, dv), lambda b, s: (cm(b, s), 0)),
        )

    sb = specs(True)
    yb = pl.pallas_call(
        functools.partial(_ret_bwd_kernel, heads=heads),
        grid=(lay.nb, nsteps),
        in_specs=[smem, sb["q"], sb["k"], sb["v"]],
        out_specs=sb["row"],
        out_shape=jax.ShapeDtypeStruct((t, dv), BF16),
        scratch_shapes=[state],
        compiler_params=_cparams("arbitrary", "arbitrary"),
        name="ret_scan_bwd",
    )(dec, qkvg, qkvg, qkvg)

    sf = specs(False)
    return pl.pallas_call(
        functools.partial(_ret_fwd_kernel, heads=heads),
        grid=(lay.nb, nsteps),
        in_specs=[smem, sf["q"], sf["k"], sf["v"], sf["g"], sf["row"]],
        out_specs=sf["row"],
        out_shape=jax.ShapeDtypeStruct((t, dv), BF16),
        scratch_shapes=[state],
        compiler_params=_cparams("arbitrary", "arbitrary"),
        name="ret_scan_fwd",
    )(dec, qkvg, qkvg, qkvg, qkvg, yb)


def _ffn_kernel(x_ref, y_ref, g1_ref, sh_ref, sc_ref, g2_ref, ng_ref, wo_ref, w1_ref, w3_ref, w2_ref, fg_ref,
                o_ref, acc_ref, *, n_ff, final_norm):
    x1 = x_ref[...] + g1_ref[...] * jnp.dot(y_ref[...], wo_ref[...], preferred_element_type=F32)
    hb = _norm_mod(x1, ng_ref[...], sh_ref[...], sc_ref[...]).astype(BF16)
    for f in range(n_ff):
        fs = slice(f * FF_CHUNK, (f + 1) * FF_CHUNK)
        a = jnp.dot(hb, w1_ref[:, fs], preferred_element_type=F32)
        b = jnp.dot(hb, w3_ref[:, fs], preferred_element_type=F32)
        part = jnp.dot((_silu(a) * b).astype(BF16), w2_ref[fs, :], preferred_element_type=F32)
        if f == 0:
            acc_ref[...] = part
        else:
            acc_ref[...] += part
    out = x1 + g2_ref[...] * acc_ref[...]
    if final_norm:
        ms = jnp.mean(out * out, axis=-1, keepdims=True)
        out = out * lax.rsqrt(ms + EPS) * fg_ref[...]
    o_ref[...] = out


def _ffn(lay, xall, y, modl, norm2_g, w_out, w1, w3, w2, final_g, *, last):
    d = xall.shape[1]
    dy = y.shape[1]
    dff = w1.shape[1]
    assert dff % FF_CHUNK == 0
    tm = _pick_tile((512, 256, 128), lay.seq, lay.t_ctx)
    t_out = lay.t_lat if last else lay.t_all
    return pl.pallas_call(
        functools.partial(_ffn_kernel, n_ff=dff // FF_CHUNK, final_norm=last),
        grid=(t_out // tm,),
        in_specs=[
            pl.BlockSpec((tm, d), lambda i: (i, 0)),
            pl.BlockSpec((tm, dy), lambda i: (i, 0)),
            _mod_spec(lay, tm, 2, d), _mod_spec(lay, tm, 3, d), _mod_spec(lay, tm, 4, d), _mod_spec(lay, tm, 5, d),
            _const_spec((1, d)),
            _const_spec((dy, d)), _const_spec((d, dff)), _const_spec((d, dff)), _const_spec((dff, d)),
            _const_spec((1, d)),
        ],
        out_specs=pl.BlockSpec((tm, d), lambda i: (i, 0)),
        out_shape=jax.ShapeDtypeStruct((t_out, d), F32),
        scratch_shapes=[pltpu.VMEM((tm, d), F32)],
        compiler_params=_cparams("arbitrary"),
        name="ffn",
    )(xall, y, modl, modl, modl, modl, norm2_g, w_out, w1, w3, w2, final_g)


def kernel(x, c, ctx, c_ctx, mod_w, mod_b, norm1_g, norm2_g, ffn_w1, ffn_w3, ffn_w2, ssd_w_in, ssd_conv_w, ssd_conv_b, ssd_a_log_f, ssd_a_log_b, ssd_dt_bias_f, ssd_dt_bias_b, ssd_d, ssd_norm_g, ssd_w_out, gmlp_w_in, gmlp_norm_g, gmlp_w_s, gmlp_b_s, gmlp_w_out, ret_w_in, ret_decay_f, ret_decay_b, ret_w_out, final_g):
    nb, seq, d = x.shape
    ctx_len = ctx.shape[1]
    depth = mod_w.shape[0]
    assert seq % CHUNK == 0 and ctx_len % CHUNK == 0 and seq % GRID_W == 0
    lay = _Layout(nb, seq, ctx_len)

    rows = -(-(nb + 1) // 8) * 8
    cvec = jnp.zeros((rows, d), F32).at[:nb].set(c).at[nb].set(c_ctx)
    mod = _modulation(cvec, mod_w, mod_b)

    xall = jnp.concatenate([x.reshape(nb * seq, d), ctx.reshape(nb * ctx_len, d)], axis=0)
    final_g2 = final_g.reshape(1, d)

    for i in range(depth):
        last = i == depth - 1
        kind, j = i % N_MIXERS, i // N_MIXERS
        modl = mod[i].reshape(rows * 6, 1, d)
        g1 = norm1_g[i].reshape(1, d)
        if kind == 0:
            heads = ssd_d.shape[1]
            inner = heads * SSD_HEAD_DIM
            n_main = 2 * inner + 2 * SSD_GROUPS * SSD_STATE
            w_in = ssd_w_in[j]
            w_dt = jnp.zeros((d, LANES), F32).at[:, :2 * heads].set(w_in[:, n_main:]).astype(BF16)
            dt_bias = jnp.zeros((1, LANES), F32).at[0, :2 * heads].set(
                jnp.concatenate([ssd_dt_bias_f[j], ssd_dt_bias_b[j]]))
            alog = jnp.zeros((1, LANES), F32).at[0, :2 * heads].set(
                jnp.concatenate([ssd_a_log_f[j], ssd_a_log_b[j]]))
            zx, dt = _ssd_proj(lay, xall, g1, modl, w_in[:, :n_main].astype(BF16), w_dt, dt_bias)
            xbc = _ssd_conv(lay, zx, ssd_conv_w[j], ssd_conv_b[j], inner)
            y = _ssd_scan(lay, zx, xbc, dt, alog, ssd_d[j], ssd_norm_g[j])
            w_out = ssd_w_out[j]
        elif kind == 1:
            uv = _gmlp_proj(lay, xall, g1, modl, gmlp_w_in[j].astype(BF16))
            y = _gmlp_gate(lay, uv, gmlp_norm_g[j], gmlp_w_s[j], gmlp_b_s[j])
            w_out = gmlp_w_out[j]
        else:
            qkvg = _ret_proj(lay, xall, g1, modl, ret_w_in[j].astype(BF16))
            y = _ret_scan(lay, qkvg, ret_decay_f[j], ret_decay_b[j], d)
            w_out = ret_w_out[j]
        xall = _ffn(lay, xall, y, modl, norm2_g[i].reshape(1, d), w_out.astype(BF16),
                    ffn_w1[i].astype(BF16), ffn_w3[i].astype(BF16), ffn_w2[i].astype(BF16), final_g2, last=last)
    return xall.reshape(nb, seq, d)
```

```python
import functools

import jax
import jax.numpy as jnp
from jax import lax
from jax.experimental import pallas as pl
from jax.experimental.pallas import tpu as pltpu

F32 = jnp.float32
BF16 = jnp.bfloat16
HI = lax.Precision.HIGHEST

EPS = 1e-6
CHUNK = 128
GRID_W = 64
ROPE_BASE = 10000.0
N_MIXERS = 3
SSD_HEAD_DIM = 64
SSD_GROUPS = 4
SSD_STATE = 128
SSD_CONV = 5
GMLP_GROUPS = 8
RET_HEADS = 4
LANES = 128
SUBLANES = 8
HALO = 16
FF_CHUNK = 256
PROJ_CHUNK = 256
VMEM_LIMIT = 56 * 1024 * 1024
LOG2E = 1.4426950408889634


def _cparams(*sem):
    return pltpu.CompilerParams(dimension_semantics=sem, vmem_limit_bytes=VMEM_LIMIT)


def _pick_tile(cands, *dims):
    for t in cands:
        if all(d % t == 0 for d in dims):
            return t
    raise ValueError(f"no tile in {cands} divides {dims}")


def _silu(x):
    return x / (1.0 + jnp.exp(-x))


def _gelu_tanh(x):
    return 0.5 * x * (1.0 + jnp.tanh(0.7978845608028654 * (x + 0.044715 * (x * x * x))))


def _softplus(x):
    return jnp.maximum(x, 0.0) + jnp.log(1.0 + jnp.exp(-jnp.abs(x)))


def _norm_mod(x, g, sh, sc):
    ms = jnp.mean(x * x, axis=-1, keepdims=True)
    y = x * lax.rsqrt(ms + EPS) * g
    return y * (1.0 + sc) + sh


def _const_spec(shape):
    nd = len(shape)
    return pl.BlockSpec(shape, lambda *_: (0,) * nd, pipeline_mode=pl.Buffered(1))


def _mod_kernel(c_ref, w_ref, b_ref, o_ref):
    a = _silu(c_ref[...])
    o_ref[...] = jnp.dot(a, w_ref[...], preferred_element_type=F32, precision=HI) + b_ref[...]


def _modulation(cvec, mod_w, mod_b):
    depth, d, d6 = mod_w.shape
    r = cvec.shape[0]
    nk = d6 // d
    return pl.pallas_call(
        _mod_kernel,
        grid=(depth, nk),
        in_specs=[
            pl.BlockSpec((r, d), lambda l, k: (0, 0)),
            pl.BlockSpec((None, d, d), lambda l, k: (l, 0, k)),
            pl.BlockSpec((None, 1, d), lambda l, k: (l, 0, k)),
        ],
        out_specs=pl.BlockSpec((None, r, d), lambda l, k: (l, 0, k)),
        out_shape=jax.ShapeDtypeStruct((depth, r, d6), F32),
        compiler_params=_cparams("arbitrary", "arbitrary"),
        name="modulation",
    )(cvec, mod_w, mod_b.reshape(depth, 1, d6))


class _Layout:
    def __init__(self, nb, seq, ctx):
        self.nb, self.seq, self.ctx = nb, seq, ctx
        self.t_lat = nb * seq
        self.t_ctx = nb * ctx
        self.t_all = self.t_lat + self.t_ctx
        self.ncl = seq // CHUNK
        self.ncc = ctx // CHUNK

    def mod_row(self, i, tm):
        return jnp.minimum((i * tm) // self.seq, self.nb)

    def chunk_block(self, b, s, reverse):
        if reverse:
            c_ctx, c_lat = self.ncc - 1 - s, self.ncl - 1 - (s - self.ncc)
        else:
            c_ctx, c_lat = s, s - self.ncc
        return jnp.where(s < self.ncc, self.nb * self.ncl + b * self.ncc + c_ctx, b * self.ncl + c_lat)


def _mod_spec(lay, tm, k, d):
    return pl.BlockSpec((None, 1, d), lambda i, *_: (lay.mod_row(i, tm) * 6 + k, 0, 0))


def _ssd_proj_kernel(x_ref, xp_ref, xn_ref, g_ref, sh_ref, sc_ref, wz_ref, wc_ref, cw_ref, cb_ref, wdt_ref, dtb_ref,
                     z_ref, xbc_ref, dt_ref, *, tm, lay, k_taps):
    i = pl.program_id(0)
    t0 = i * tm
    in_lat = t0 < lay.t_lat
    off = jnp.where(in_lat, t0, t0 - lay.t_lat)
    length = jnp.where(in_lat, lay.seq, lay.ctx)
    at_start = (off % length) == 0
    at_end = ((off + tm) % length) == 0
    g, sh, sc = g_ref[...], sh_ref[...], sc_ref[...]
    hm = _norm_mod(x_ref[...], g, sh, sc).astype(BF16)
    hp = jnp.where(at_start, 0.0, _norm_mod(xp_ref[...], g, sh, sc)).astype(BF16)
    hn = jnp.where(at_end, 0.0, _norm_mod(xn_ref[...], g, sh, sc)).astype(BF16)
    h_ext = jnp.concatenate([hp, hm, hn], axis=0)
    n_ext = tm + 2 * HALO
    for j in range(z_ref.shape[1] // PROJ_CHUNK):
        cs = slice(j * PROJ_CHUNK, (j + 1) * PROJ_CHUNK)
        z_ref[:, cs] = jnp.dot(hm, wz_ref[:, cs], preferred_element_type=F32).astype(BF16)
    mid = k_taps // 2
    nv, hv = n_ext // SUBLANES, HALO // SUBLANES
    sub = lax.broadcasted_iota(jnp.int32, (nv, SUBLANES, PROJ_CHUNK), 1)

    def shifted(a3, d):
        r = pltpu.roll(a3, (-d) % SUBLANES, 1)
        if d > 0:
            other, own = jnp.concatenate([r[1:], r[:1]], axis=0), sub < SUBLANES - d
        else:
            other, own = jnp.concatenate([r[-1:], r[:-1]], axis=0), sub >= -d
        return jnp.where(own, r, other)[hv:hv + tm // SUBLANES].reshape(tm, PROJ_CHUNK)

    for j in range(xbc_ref.shape[1] // PROJ_CHUNK):
        cs = slice(j * PROJ_CHUNK, (j + 1) * PROJ_CHUNK)
        acc = jnp.dot(h_ext, wc_ref[:, cs], preferred_element_type=F32)
        a3 = acc.reshape(nv, SUBLANES, PROJ_CHUNK)
        y = cb_ref[:, cs] + cw_ref[mid:mid + 1, cs] * acc[HALO:HALO + tm]
        for k in range(k_taps):
            if k != mid:
                y = y + cw_ref[k:k + 1, cs] * shifted(a3, k - mid)
        xbc_ref[:, cs] = _silu(y).astype(BF16)
    dr = jnp.dot(hm, wdt_ref[...], preferred_element_type=F32) + dtb_ref[...]
    dt_ref[...] = _softplus(dr)


def _gmlp_proj_kernel(x_ref, g_ref, sh_ref, sc_ref, w_ref, o_ref, *, n_chunks):
    hb = _norm_mod(x_ref[...], g_ref[...], sh_ref[...], sc_ref[...]).astype(BF16)
    for j in range(n_chunks):
        cs = slice(j * PROJ_CHUNK, (j + 1) * PROJ_CHUNK)
        acc = jnp.dot(hb, w_ref[:, cs], preferred_element_type=F32)
        o_ref[:, cs] = _gelu_tanh(acc).astype(BF16)


def _ret_proj_kernel(x_ref, g_ref, sh_ref, sc_ref, w_ref, cos_ref, sin_ref, o_ref, *, n_chunks, n_heads, k_scale):
    hb = _norm_mod(x_ref[...], g_ref[...], sh_ref[...], sc_ref[...]).astype(BF16)
    half = LANES // 2
    for j in range(n_chunks):
        cs = slice(j * PROJ_CHUNK, (j + 1) * PROJ_CHUNK)
        acc = jnp.dot(hb, w_ref[:, cs], preferred_element_type=F32)
        if j < 2 * n_heads:
            rolled = jnp.concatenate(
                [pltpu.roll(acc[:, :LANES], half, 1), pltpu.roll(acc[:, LANES:], half, 1)], axis=1)
            acc = acc * cos_ref[...] + rolled * sin_ref[...]
            if j >= n_heads:
                acc = acc * k_scale
        o_ref[:, cs] = acc.astype(BF16)


def _proj_common_specs(lay, tm, d, n_out):
    return [
        pl.BlockSpec((tm, d), lambda i: (i, 0)),
        _const_spec((1, d)),
        _mod_spec(lay, tm, 0, d),
        _mod_spec(lay, tm, 1, d),
        _const_spec((d, n_out)),
    ]


def _ssd_proj(lay, xall, g, modl, w_z, w_xbc, conv_w, conv_b, w_dt, dt_bias):
    t, d = xall.shape
    nz, nc = w_z.shape[1], w_xbc.shape[1]
    k_taps = conv_w.shape[0]
    assert k_taps // 2 <= HALO
    tm = _pick_tile((512, 256, 128), lay.seq, lay.ctx)
    hb = tm // HALO
    last_hb = t // HALO - 1
    return pl.pallas_call(
        functools.partial(_ssd_proj_kernel, tm=tm, lay=lay, k_taps=k_taps),
        grid=(t // tm,),
        in_specs=[
            pl.BlockSpec((tm, d), lambda i: (i, 0)),
            pl.BlockSpec((HALO, d), lambda i: (jnp.maximum(i * hb - 1, 0), 0)),
            pl.BlockSpec((HALO, d), lambda i: (jnp.minimum((i + 1) * hb, last_hb), 0)),
            _const_spec((1, d)), _mod_spec(lay, tm, 0, d), _mod_spec(lay, tm, 1, d),
            _const_spec((d, nz)), _const_spec((d, nc)), _const_spec((k_taps, nc)), _const_spec((1, nc)),
            _const_spec((d, LANES)), _const_spec((1, LANES)),
        ],
        out_specs=[pl.BlockSpec((tm, nz), lambda i: (i, 0)), pl.BlockSpec((tm, nc), lambda i: (i, 0)),
                   pl.BlockSpec((tm, LANES), lambda i: (i, 0))],
        out_shape=[jax.ShapeDtypeStruct((t, nz), BF16), jax.ShapeDtypeStruct((t, nc), BF16),
                   jax.ShapeDtypeStruct((t, LANES), F32)],
        compiler_params=_cparams("arbitrary"),
        name="ssd_proj",
    )(xall, xall, xall, g, modl, modl, w_z, w_xbc, conv_w, conv_b.reshape(1, nc), w_dt, dt_bias)


def _gmlp_proj(lay, xall, g, modl, w_in):
    t, d = xall.shape
    n_out = w_in.shape[1]
    tm = _pick_tile((512, 256, 128), lay.seq, lay.t_ctx)
    return pl.pallas_call(
        functools.partial(_gmlp_proj_kernel, n_chunks=n_out // PROJ_CHUNK),
        grid=(t // tm,),
        in_specs=_proj_common_specs(lay, tm, d, n_out),
        out_specs=pl.BlockSpec((tm, n_out), lambda i: (i, 0)),
        out_shape=jax.ShapeDtypeStruct((t, n_out), BF16),
        compiler_params=_cparams("arbitrary"),
        name="gmlp_proj",
    )(xall, g, modl, modl, w_in)


def _rope_tables(lay, tm, head_dim):
    quarter = head_dim // 4
    pos = jnp.arange(lay.seq)
    inv = ROPE_BASE ** (-jnp.arange(quarter, dtype=F32) / quarter)
    ang_r = (pos // GRID_W).astype(F32)[:, None] * inv
    ang_c = (pos % GRID_W).astype(F32)[:, None] * inv
    cos = jnp.concatenate([jnp.cos(ang_r)] * 2 + [jnp.cos(ang_c)] * 2, axis=1)
    sin = jnp.concatenate([-jnp.sin(ang_r), jnp.sin(ang_r), -jnp.sin(ang_c), jnp.sin(ang_c)], axis=1)
    cos = jnp.concatenate([cos, jnp.ones((tm, head_dim), F32)], axis=0)
    sin = jnp.concatenate([sin, jnp.zeros((tm, head_dim), F32)], axis=0)
    return cos, sin


def _ret_proj(lay, xall, g, modl, w_in):
    t, d = xall.shape
    n_out = w_in.shape[1]
    head_dim = d // RET_HEADS
    assert head_dim == PROJ_CHUNK
    tm = _pick_tile((512, 256, 128), lay.seq, lay.t_ctx)
    cos, sin = _rope_tables(lay, tm, head_dim)
    lat_tiles = lay.seq // tm

    def tab_idx(i):
        return (jnp.where(i * tm < lay.t_lat, i % lat_tiles, lat_tiles), 0)

    return pl.pallas_call(
        functools.partial(_ret_proj_kernel, n_chunks=n_out // PROJ_CHUNK, n_heads=RET_HEADS,
                          k_scale=head_dim ** -0.5),
        grid=(t // tm,),
        in_specs=_proj_common_specs(lay, tm, d, n_out) + [
            pl.BlockSpec((tm, head_dim), tab_idx), pl.BlockSpec((tm, head_dim), tab_idx)],
        out_specs=pl.BlockSpec((tm, n_out), lambda i: (i, 0)),
        out_shape=jax.ShapeDtypeStruct((t, n_out), BF16),
        compiler_params=_cparams("arbitrary"),
        name="ret_proj",
    )(xall, g, modl, modl, w_in, cos, sin)


def _tri_masks(q):
    ri = lax.broadcasted_iota(jnp.int32, (q, q), 0)
    ci = lax.broadcasted_iota(jnp.int32, (q, q), 1)
    return ri >= ci, ri <= ci


def _ssd_cums(dt, alog_ref):
    q = dt.shape[0]
    tril, triu = _tri_masks(q)
    a = dt * (-LOG2E * jnp.exp(alog_ref[...]))
    cum_f = jnp.dot(tril.astype(F32), a, preferred_element_type=F32, precision=HI)
    cum_b = jnp.dot(triu.astype(F32), a, preferred_element_type=F32, precision=HI)
    return cum_f, cum_b, tril, triu


def _ssd_bwd_kernel(x_ref, b_ref, c_ref, dt_ref, alog_ref, expb_ref, yb_ref, s_ref, *, groups):
    @pl.when(pl.program_id(1) == 0)
    def _():
        s_ref[...] = jnp.zeros_like(s_ref)

    q = dt_ref.shape[0]
    gw = x_ref.shape[1] // groups
    n = b_ref.shape[1] // groups
    dt = dt_ref[...]
    _, cum_b, _, _ = _ssd_cums(dt, alog_ref)
    expb = expb_ref[...]
    e_in = jnp.dot(jnp.exp2(cum_b).astype(BF16), expb, preferred_element_type=F32)
    w_in = jnp.dot((dt * jnp.exp2(cum_b[0:1, :] - cum_b)).astype(BF16), expb, preferred_element_type=F32)
    for g in range(groups):
        gs = slice(g * gw, (g + 1) * gw)
        ns = slice(g * n, (g + 1) * n)
        s_g = s_ref[:, gs]
        y = jnp.dot(c_ref[:, ns], s_g.astype(BF16), preferred_element_type=F32) * e_in[:, gs]
        yb_ref[:, gs] = y.astype(BF16)
        xw = (x_ref[:, gs].astype(F32) * w_in[:, gs]).astype(BF16)
        upd = lax.dot_general(b_ref[:, ns], xw, (((0,), (0,)), ((), ())), preferred_element_type=F32)
        s_ref[:, gs] = s_g * e_in[0:1, gs] + upd


def _ssd_fwd_kernel(x_ref, b_ref, c_ref, dt_ref, z_ref, yb_ref, alog_ref, expf_ref, expb_ref, dskip_ref, ng_ref,
                    o_ref, s_ref, y_ref, *, groups, hpg, p):
    @pl.when(pl.program_id(1) == 0)
    def _():
        s_ref[...] = jnp.zeros_like(s_ref)

    q = dt_ref.shape[0]
    inner = x_ref.shape[1]
    gw = inner // groups
    n = b_ref.shape[1] // groups
    heads = groups * hpg
    dt = dt_ref[...]
    cum_f, cum_b, tril, _ = _ssd_cums(dt, alog_ref)
    ldt = jnp.log(dt) * LOG2E
    row_f = (ldt - cum_f).T
    row_b = (ldt - cum_b).T
    expf = expf_ref[...]
    e_f = jnp.dot(jnp.exp2(cum_f).astype(BF16), expf, preferred_element_type=F32)
    w_f = jnp.dot((dt * jnp.exp2(cum_f[q - 1:q, :] - cum_f)).astype(BF16), expf, preferred_element_type=F32)
    lane = lax.broadcasted_iota(jnp.int32, (q, 2 * p), 1)
    left = lane < p
    eye = lax.broadcasted_iota(jnp.int32, (q, q), 0) == lax.broadcasted_iota(jnp.int32, (q, q), 1)
    ssq = jnp.zeros((q, 1), F32)
    for g in range(groups):
        gs = slice(g * gw, (g + 1) * gw)
        ns = slice(g * n, (g + 1) * n)
        c_g = c_ref[:, ns]
        b_g = b_ref[:, ns]
        scores = lax.dot_general(c_g, b_g, (((1,), (1,)), ((), ())), preferred_element_type=F32)
        s_g = s_ref[:, gs]
        pieces = []
        for pr in range(hpg // 2):
            ms = []
            for r in (g * hpg + 2 * pr, g * hpg + 2 * pr + 1):
                rb = heads + r
                arg = jnp.where(tril, cum_f[:, r:r + 1] + row_f[r:r + 1, :], cum_b[:, rb:rb + 1] + row_b[rb:rb + 1, :])
                ms.append((scores * jnp.exp2(arg)).astype(BF16))
            lhs = jnp.concatenate(ms, axis=1)
            ps = slice(g * gw + pr * 2 * p, g * gw + (pr + 1) * 2 * p)
            xp = x_ref[:, ps]
            zero = jnp.zeros_like(xp)
            rhs = jnp.concatenate([jnp.where(left, xp, zero), jnp.where(left, zero, xp)], axis=0)
            pieces.append(jnp.dot(lhs, rhs, preferred_element_type=F32))
        x_g = x_ref[:, gs].astype(F32)
        diag = jnp.sum(jnp.where(eye, scores, 0.0), axis=-1, keepdims=True)
        skip = dskip_ref[:, gs] + jnp.dot((diag * dt).astype(BF16), expb_ref[:, gs], preferred_element_type=F32)
        y = jnp.concatenate(pieces, axis=1)
        y = y + jnp.dot(c_g, s_g.astype(BF16), preferred_element_type=F32) * e_f[:, gs]
        y = y + yb_ref[:, gs].astype(F32) + skip * x_g
        y = y * _silu(z_ref[:, gs].astype(F32))
        ssq = ssq + jnp.sum(y * y, axis=-1, keepdims=True)
        y_ref[:, gs] = y
        xw = (x_g * w_f[:, gs]).astype(BF16)
        upd = lax.dot_general(b_g, xw, (((0,), (0,)), ((), ())), preferred_element_type=F32)
        s_ref[:, gs] = s_g * e_f[q - 1:q, gs] + upd
    rms = lax.rsqrt(ssq / inner + EPS)
    o_ref[...] = (y_ref[...] * rms * ng_ref[...]).astype(BF16)


def _ssd_scan(lay, zx, xbc, dt, alog, d_skip, norm_g):
    t = zx.shape[0]
    inner = d_skip.shape[0] * SSD_HEAD_DIM
    heads = d_skip.shape[0]
    hpg = heads // SSD_GROUPS
    gn = SSD_GROUPS * SSD_STATE
    assert 2 * heads <= LANES and inner % gn == 0 and hpg % 2 == 0
    nsteps = lay.ncc + lay.ncl
    xb, bb, cb = 0, inner // gn, inner // gn + 1

    lane_head = jnp.arange(LANES)[:, None]
    col_head = (jnp.arange(inner) // SSD_HEAD_DIM)[None, :]
    exp_f = (lane_head == col_head).astype(BF16)
    exp_b = (lane_head == col_head + heads).astype(BF16)
    dskip = jnp.repeat(d_skip, SSD_HEAD_DIM).reshape(1, inner)

    def specs(reverse):
        cm = lambda b, s: lay.chunk_block(b, s, reverse)
        return dict(
            x=pl.BlockSpec((CHUNK, inner), lambda b, s: (cm(b, s), xb)),
            b=pl.BlockSpec((CHUNK, gn), lambda b, s: (cm(b, s), bb)),
            c=pl.BlockSpec((CHUNK, gn), lambda b, s: (cm(b, s), cb)),
            dt=pl.BlockSpec((CHUNK, LANES), lambda b, s: (cm(b, s), 0)),
            row=pl.BlockSpec((CHUNK, inner), lambda b, s: (cm(b, s), 0)),
        )

    sb = specs(True)
    yb = pl.pallas_call(
        functools.partial(_ssd_bwd_kernel, groups=SSD_GROUPS),
        grid=(lay.nb, nsteps),
        in_specs=[sb["x"], sb["b"], sb["c"], sb["dt"], _const_spec((1, LANES)), _const_spec((LANES, inner))],
        out_specs=sb["row"],
        out_shape=jax.ShapeDtypeStruct((t, inner), BF16),
        scratch_shapes=[pltpu.VMEM((SSD_STATE, inner), F32)],
        compiler_params=_cparams("arbitrary", "arbitrary"),
        name="ssd_scan_bwd",
    )(xbc, xbc, xbc, dt, alog, exp_b)

    sf = specs(False)
    return pl.pallas_call(
        functools.partial(_ssd_fwd_kernel, groups=SSD_GROUPS, hpg=hpg, p=SSD_HEAD_DIM),
        grid=(lay.nb, nsteps),
        in_specs=[sf["x"], sf["b"], sf["c"], sf["dt"], sf["row"], sf["row"], _const_spec((1, LANES)),
                  _const_spec((LANES, inner)), _const_spec((LANES, inner)), _const_spec((1, inner)),
                  _const_spec((1, inner))],
        out_specs=sf["row"],
        out_shape=jax.ShapeDtypeStruct((t, inner), BF16),
        scratch_shapes=[pltpu.VMEM((SSD_STATE, inner), F32), pltpu.VMEM((CHUNK, inner), F32)],
        compiler_params=_cparams("arbitrary", "arbitrary"),
        name="ssd_scan_fwd",
    )(xbc, xbc, xbc, dt, zx, yb, alog, exp_f, exp_b, dskip, norm_g.reshape(1, inner))


def _gmlp_gate_kernel(u_ref, v_ref, ng_ref, ws_ref, bias_ref, o_ref, *, groups):
    tm, width = u_ref.shape
    gd = width // groups
    v = v_ref[...].astype(F32)
    ms = jnp.mean(v * v, axis=-1, keepdims=True)
    vn = (v * lax.rsqrt(ms + EPS) * ng_ref[...]).astype(BF16)
    for c in range(tm // CHUNK):
        rs = slice(c * CHUNK, (c + 1) * CHUNK)
        for g in range(groups):
            gs = slice(g * gd, (g + 1) * gd)
            mixed = jnp.dot(ws_ref[g], vn[rs, gs], preferred_element_type=F32) + bias_ref[:, gs]
            o_ref[rs, gs] = (u_ref[rs, gs].astype(F32) * mixed).astype(BF16)


def _gmlp_gate(lay, uv, norm_g, w_s, b_s):
    t = uv.shape[0]
    width = uv.shape[1] // 2
    groups = w_s.shape[0]
    tm = _pick_tile((512, 256, 128), lay.seq, lay.t_ctx)
    bias = jnp.repeat(b_s.T, width // groups, axis=1)
    return pl.pallas_call(
        functools.partial(_gmlp_gate_kernel, groups=groups),
        grid=(t // tm,),
        in_specs=[
            pl.BlockSpec((tm, width), lambda i: (i, 0)),
            pl.BlockSpec((tm, width), lambda i: (i, 1)),
            _const_spec((1, width)),
            _const_spec((groups, CHUNK, CHUNK)),
            _const_spec((CHUNK, width)),
        ],
        out_specs=pl.BlockSpec((tm, width), lambda i: (i, 0)),
        out_shape=jax.ShapeDtypeStruct((t, width), BF16),
        compiler_params=_cparams("arbitrary"),
        name="gmlp_gate",
    )(uv, uv, norm_g.reshape(1, width), w_s.astype(BF16), bias)


def _ret_decay_tiles(dec_ref, row, h, q):
    return -jnp.exp(jnp.full((q, q), dec_ref[row, h], F32))


def _ret_bwd_kernel(dec_ref, q_ref, k_ref, v_ref, yb_ref, s_ref, *, heads):
    @pl.when(pl.program_id(1) == 0)
    def _():
        s_ref[...] = jnp.zeros_like(s_ref)

    q = q_ref.shape[0]
    dk = q_ref.shape[1] // heads
    dv = v_ref.shape[1] // heads
    pos = lax.broadcasted_iota(jnp.int32, (q, q), 0).astype(F32)
    for h in range(heads):
        lg = _ret_decay_tiles(dec_ref, 1, h, q)
        e_in = jnp.exp(lg * (q - pos))
        tail = jnp.exp(lg * pos)
        s_h = s_ref[h]
        ks = slice(h * dk, (h + 1) * dk)
        vs = slice(h * dv, (h + 1) * dv)
        y = jnp.dot(q_ref[:, ks], s_h.astype(BF16), preferred_element_type=F32)
        yb_ref[:, vs] = (y * jnp.concatenate([e_in] * (dv // q), axis=1)).astype(BF16)
        kt = (k_ref[:, ks].astype(F32) * jnp.concatenate([tail] * (dk // q), axis=1)).astype(BF16)
        upd = lax.dot_general(kt, v_ref[:, vs], (((0,), (0,)), ((), ())), preferred_element_type=F32)
        s_ref[h] = s_h * jnp.concatenate([jnp.exp(lg[0:1, :] * q)] * (dv // q), axis=1) + upd


def _ret_fwd_kernel(dec_ref, q_ref, k_ref, v_ref, g_ref, yb_ref, o_ref, s_ref, *, heads):
    @pl.when(pl.program_id(1) == 0)
    def _():
        s_ref[...] = jnp.zeros_like(s_ref)

    q = q_ref.shape[0]
    dk = q_ref.shape[1] // heads
    dv = v_ref.shape[1] // heads
    pos = lax.broadcasted_iota(jnp.int32, (q, q), 0).astype(F32)
    diff = pos - lax.broadcasted_iota(jnp.int32, (q, q), 1).astype(F32)
    for h in range(heads):
        lg_f = _ret_decay_tiles(dec_ref, 0, h, q)
        lg_b = _ret_decay_tiles(dec_ref, 1, h, q)
        dec = (jnp.where(diff >= 0, jnp.exp(lg_f * jnp.maximum(diff, 0.0)), 0.0)
               + jnp.where(diff <= 0, jnp.exp(lg_b * jnp.maximum(-diff, 0.0)), 0.0))
        e_in = jnp.exp(lg_f * (pos + 1.0))
        tail = jnp.exp(lg_f * (q - 1.0 - pos))
        s_h = s_ref[h]
        ks = slice(h * dk, (h + 1) * dk)
        vs = slice(h * dv, (h + 1) * dv)
        q_h, k_h, v_h = q_ref[:, ks], k_ref[:, ks], v_ref[:, vs]
        scores = lax.dot_general(q_h, k_h, (((1,), (1,)), ((), ())), preferred_element_type=F32)
        y = jnp.dot((scores * dec).astype(BF16), v_h, preferred_element_type=F32)
        y = y + jnp.dot(q_h, s_h.astype(BF16), preferred_element_type=F32) * jnp.concatenate(
            [e_in] * (dv // q), axis=1)
        y = y + yb_ref[:, vs].astype(F32)
        y = y * lax.rsqrt(jnp.mean(y * y, axis=-1, keepdims=True) + EPS)
        o_ref[:, vs] = (_silu(g_ref[:, vs].astype(F32)) * y).astype(BF16)
        kt = (k_h.astype(F32) * jnp.concatenate([tail] * (dk // q), axis=1)).astype(BF16)
        upd = lax.dot_general(kt, v_h, (((0,), (0,)), ((), ())), preferred_element_type=F32)
        s_ref[h] = s_h * jnp.concatenate([jnp.exp(lg_f[0:1, :] * q)] * (dv // q), axis=1) + upd


def _ret_scan(lay, qkvg, decay_f, decay_b, d):
    t = qkvg.shape[0]
    heads = decay_f.shape[0]
    dv = 2 * d
    nsteps = lay.ncc + lay.ncl
    dec = jnp.stack([decay_f, decay_b]).astype(F32)
    smem = pl.BlockSpec(memory_space=pltpu.SMEM)
    state = pltpu.VMEM((heads, d // heads, dv // heads), F32)

    def specs(reverse):
        cm = lambda b, s: lay.chunk_block(b, s, reverse)
        return dict(
            q=pl.BlockSpec((CHUNK, d), lambda b, s: (cm(b, s), 0)),
            k=pl.BlockSpec((CHUNK, d), lambda b, s: (cm(b, s), 1)),
            v=pl.BlockSpec((CHUNK, dv), lambda b, s: (cm(b, s), 1)),
            g=pl.BlockSpec((CHUNK, dv), lambda b, s: (cm(b, s), 2)),
            row=pl.BlockSpec((CHUNK, dv), lambda b, s: (cm(b, s), 0)),
        )

    sb = specs(True)
    yb = pl.pallas_call(
        functools.partial(_ret_bwd_kernel, heads=heads),
        grid=(lay.nb, nsteps),
        in_specs=[smem, sb["q"], sb["k"], sb["v"]],
        out_specs=sb["row"],
        out_shape=jax.ShapeDtypeStruct((t, dv), BF16),
        scratch_shapes=[state],
        compiler_params=_cparams("arbitrary", "arbitrary"),
        name="ret_scan_bwd",
    )(dec, qkvg, qkvg, qkvg)

    sf = specs(False)
    return pl.pallas_call(
        functools.partial(_ret_fwd_kernel, heads=heads),
        grid=(lay.nb, nsteps),
        in_specs=[smem, sf["q"], sf["k"], sf["v"], sf["g"], sf["row"]],
        out_specs=sf["row"],
        out_shape=jax.ShapeDtypeStruct((t, dv), BF16),
        scratch_shapes=[state],
        compiler_params=_cparams("arbitrary", "arbitrary"),
        name="ret_scan_fwd",
    )(dec, qkvg, qkvg, qkvg, qkvg, yb)


def _ffn_kernel(x_ref, y_ref, g1_ref, sh_ref, sc_ref, g2_ref, ng_ref, wo_ref, w1_ref, w3_ref, w2_ref, fg_ref,
                o_ref, acc_ref, *, n_ff, final_norm):
    x1 = x_ref[...] + g1_ref[...] * jnp.dot(y_ref[...], wo_ref[...], preferred_element_type=F32)
    hb = _norm_mod(x1, ng_ref[...], sh_ref[...], sc_ref[...]).astype(BF16)
    for f in range(n_ff):
        fs = slice(f * FF_CHUNK, (f + 1) * FF_CHUNK)
        a = jnp.dot(hb, w1_ref[:, fs], preferred_element_type=F32)
        b = jnp.dot(hb, w3_ref[:, fs], preferred_element_type=F32)
        part = jnp.dot((_silu(a) * b).astype(BF16), w2_ref[fs, :], preferred_element_type=F32)
        if f == 0:
            acc_ref[...] = part
        else:
            acc_ref[...] += part
    out = x1 + g2_ref[...] * acc_ref[...]
    if final_norm:
        ms = jnp.mean(out * out, axis=-1, keepdims=True)
        out = out * lax.rsqrt(ms + EPS) * fg_ref[...]
    o_ref[...] = out


def _ffn(lay, xall, y, modl, norm2_g, w_out, w1, w3, w2, final_g, *, last):
    d = xall.shape[1]
    dy = y.shape[1]
    dff = w1.shape[1]
    assert dff % FF_CHUNK == 0
    tm = _pick_tile((512, 256, 128), lay.seq, lay.t_ctx)
    t_out = lay.t_lat if last else lay.t_all
    return pl.pallas_call(
        functools.partial(_ffn_kernel, n_ff=dff // FF_CHUNK, final_norm=last),
        grid=(t_out // tm,),
        in_specs=[
            pl.BlockSpec((tm, d), lambda i: (i, 0)),
            pl.BlockSpec((tm, dy), lambda i: (i, 0)),
            _mod_spec(lay, tm, 2, d), _mod_spec(lay, tm, 3, d), _mod_spec(lay, tm, 4, d), _mod_spec(lay, tm, 5, d),
            _const_spec((1, d)),
            _const_spec((dy, d)), _const_spec((d, dff)), _const_spec((d, dff)), _const_spec((dff, d)),
            _const_spec((1, d)),
        ],
        out_specs=pl.BlockSpec((tm, d), lambda i: (i, 0)),
        out_shape=jax.ShapeDtypeStruct((t_out, d), F32),
        scratch_shapes=[pltpu.VMEM((tm, d), F32)],
        compiler_params=_cparams("arbitrary"),
        name="ffn",
    )(xall, y, modl, modl, modl, modl, norm2_g, w_out, w1, w3, w2, final_g)


def kernel(x, c, ctx, c_ctx, mod_w, mod_b, norm1_g, norm2_g, ffn_w1, ffn_w3, ffn_w2, ssd_w_in, ssd_conv_w, ssd_conv_b, ssd_a_log_f, ssd_a_log_b, ssd_dt_bias_f, ssd_dt_bias_b, ssd_d, ssd_norm_g, ssd_w_out, gmlp_w_in, gmlp_norm_g, gmlp_w_s, gmlp_b_s, gmlp_w_out, ret_w_in, ret_decay_f, ret_decay_b, ret_w_out, final_g):
    nb, seq, d = x.shape
    ctx_len = ctx.shape[1]
    depth = mod_w.shape[0]
    assert seq % CHUNK == 0 and ctx_len % CHUNK == 0 and seq % GRID_W == 0
    lay = _Layout(nb, seq, ctx_len)

    rows = -(-(nb + 1) // 8) * 8
    cvec = jnp.zeros((rows, d), F32).at[:nb].set(c).at[nb].set(c_ctx)
    mod = _modulation(cvec, mod_w, mod_b)

    xall = jnp.concatenate([x.reshape(nb * seq, d), ctx.reshape(nb * ctx_len, d)], axis=0)
    final_g2 = final_g.reshape(1, d)

    for i in range(depth):
        last = i == depth - 1
        kind, j = i % N_MIXERS, i // N_MIXERS
        modl = mod[i].reshape(rows * 6, 1, d)
        g1 = norm1_g[i].reshape(1, d)
        if kind == 0:
            heads = ssd_d.shape[1]
            inner = heads * SSD_HEAD_DIM
            n_main = 2 * inner + 2 * SSD_GROUPS * SSD_STATE
            w_in = ssd_w_in[j]
            w_dt = jnp.zeros((d, LANES), F32).at[:, :2 * heads].set(w_in[:, n_main:]).astype(BF16)
            dt_bias = jnp.zeros((1, LANES), F32).at[0, :2 * heads].set(
                jnp.concatenate([ssd_dt_bias_f[j], ssd_dt_bias_b[j]]))
            alog = jnp.zeros((1, LANES), F32).at[0, :2 * heads].set(
                jnp.concatenate([ssd_a_log_f[j], ssd_a_log_b[j]]))
            zx, xbc, dt = _ssd_proj(lay, xall, g1, modl, w_in[:, :inner].astype(BF16),
                                    w_in[:, inner:n_main].astype(BF16), ssd_conv_w[j], ssd_conv_b[j], w_dt, dt_bias)
            y = _ssd_scan(lay, zx, xbc, dt, alog, ssd_d[j], ssd_norm_g[j])
            w_out = ssd_w_out[j]
        elif kind == 1:
            uv = _gmlp_proj(lay, xall, g1, modl, gmlp_w_in[j].astype(BF16))
            y = _gmlp_gate(lay, uv, gmlp_norm_g[j], gmlp_w_s[j], gmlp_b_s[j])
            w_out = gmlp_w_out[j]
        else:
            qkvg = _ret_proj(lay, xall, g1, modl, ret_w_in[j].astype(BF16))
            y = _ret_scan(lay, qkvg, ret_decay_f[j], ret_decay_b[j], d)
            w_out = ret_w_out[j]
        xall = _ffn(lay, xall, y, modl, norm2_g[i].reshape(1, d), w_out.astype(BF16),
                    ffn_w1[i].astype(BF16), ffn_w3[i].astype(BF16), ffn_w2[i].astype(BF16), final_g2, last=last)
    return xall.reshape(nb, seq, d)
```

```python
import functools

import jax
import jax.numpy as jnp
from jax import lax
from jax.experimental import pallas as pl
from jax.experimental.pallas import tpu as pltpu

F32 = jnp.float32
BF16 = jnp.bfloat16
HI = lax.Precision.HIGHEST

EPS = 1e-6
CHUNK = 128
GRID_W = 64
ROPE_BASE = 10000.0
N_MIXERS = 3
SSD_HEAD_DIM = 64
SSD_GROUPS = 4
SSD_STATE = 128
SSD_CONV = 5
GMLP_GROUPS = 8
RET_HEADS = 4
LANES = 128
HALO = 16
FF_CHUNK = 256
PROJ_CHUNK = 256
VMEM_LIMIT = 56 * 1024 * 1024
LOG2E = 1.4426950408889634


def _cparams(*sem):
    return pltpu.CompilerParams(dimension_semantics=sem, vmem_limit_bytes=VMEM_LIMIT)


def _pick_tile(cands, *dims):
    for t in cands:
        if all(d % t == 0 for d in dims):
            return t
    raise ValueError(f"no tile in {cands} divides {dims}")


def _silu(x):
    return x / (1.0 + jnp.exp(-x))


def _gelu_tanh(x):
    return 0.5 * x * (1.0 + jnp.tanh(0.7978845608028654 * (x + 0.044715 * (x * x * x))))


def _softplus(x):
    return jnp.maximum(x, 0.0) + jnp.log(1.0 + jnp.exp(-jnp.abs(x)))


def _norm_mod(x, g, sh, sc):
    ms = jnp.mean(x * x, axis=-1, keepdims=True)
    y = x * lax.rsqrt(ms + EPS) * g
    return y * (1.0 + sc) + sh


def _const_spec(shape):
    nd = len(shape)
    return pl.BlockSpec(shape, lambda *_: (0,) * nd, pipeline_mode=pl.Buffered(1))


def _mod_kernel(c_ref, w_ref, b_ref, o_ref):
    a = _silu(c_ref[...])
    o_ref[...] = jnp.dot(a, w_ref[...], preferred_element_type=F32, precision=HI) + b_ref[...]


def _modulation(cvec, mod_w, mod_b):
    depth, d, d6 = mod_w.shape
    r = cvec.shape[0]
    nk = d6 // d
    return pl.pallas_call(
        _mod_kernel,
        grid=(depth, nk),
        in_specs=[
            pl.BlockSpec((r, d), lambda l, k: (0, 0)),
            pl.BlockSpec((None, d, d), lambda l, k: (l, 0, k)),
            pl.BlockSpec((None, 1, d), lambda l, k: (l, 0, k)),
        ],
        out_specs=pl.BlockSpec((None, r, d), lambda l, k: (l, 0, k)),
        out_shape=jax.ShapeDtypeStruct((depth, r, d6), F32),
        compiler_params=_cparams("arbitrary", "arbitrary"),
        name="modulation",
    )(cvec, mod_w, mod_b.reshape(depth, 1, d6))


class _Layout:
    def __init__(self, nb, seq, ctx):
        self.nb, self.seq, self.ctx = nb, seq, ctx
        self.t_lat = nb * seq
        self.t_ctx = nb * ctx
        self.t_all = self.t_lat + self.t_ctx
        self.ncl = seq // CHUNK
        self.ncc = ctx // CHUNK

    def mod_row(self, i, tm):
        return jnp.minimum((i * tm) // self.seq, self.nb)

    def chunk_block(self, b, s, reverse):
        if reverse:
            c_ctx, c_lat = self.ncc - 1 - s, self.ncl - 1 - (s - self.ncc)
        else:
            c_ctx, c_lat = s, s - self.ncc
        return jnp.where(s < self.ncc, self.nb * self.ncl + b * self.ncc + c_ctx, b * self.ncl + c_lat)


def _mod_spec(lay, tm, k, d):
    return pl.BlockSpec((None, 1, d), lambda i, *_: (lay.mod_row(i, tm) * 6 + k, 0, 0))


def _ssd_proj_kernel(x_ref, xp_ref, xn_ref, g_ref, sh_ref, sc_ref, wz_ref, wc_ref, cw_ref, cb_ref, wdt_ref, dtb_ref,
                     z_ref, xbc_ref, dt_ref, *, tm, lay, k_taps):
    i = pl.program_id(0)
    t0 = i * tm
    in_lat = t0 < lay.t_lat
    off = jnp.where(in_lat, t0, t0 - lay.t_lat)
    length = jnp.where(in_lat, lay.seq, lay.ctx)
    at_start = (off % length) == 0
    at_end = ((off + tm) % length) == 0
    g, sh, sc = g_ref[...], sh_ref[...], sc_ref[...]
    hm = _norm_mod(x_ref[...], g, sh, sc).astype(BF16)
    hp = jnp.where(at_start, 0.0, _norm_mod(xp_ref[...], g, sh, sc)).astype(BF16)
    hn = jnp.where(at_end, 0.0, _norm_mod(xn_ref[...], g, sh, sc)).astype(BF16)
    h_ext = jnp.concatenate([hp, hm, hn], axis=0)
    for j in range(z_ref.shape[1] // PROJ_CHUNK):
        cs = slice(j * PROJ_CHUNK, (j + 1) * PROJ_CHUNK)
        z_ref[:, cs] = jnp.dot(hm, wz_ref[:, cs], preferred_element_type=F32).astype(BF16)
    mid = k_taps // 2
    inner_bounds = list(range(lay.ctx, tm, lay.ctx))
    fix_row = lax.broadcasted_iota(jnp.int32, (2 * HALO, 1), 0)

    def conv(acc, row0, rows, masks):
        y = cb_ref[:, cs]
        for k in range(k_taps):
            term = cw_ref[k:k + 1, cs] * acc[row0 + k - mid:row0 + k - mid + rows]
            y = y + (term if masks is None else jnp.where(masks[k], term, 0.0))
        return _silu(y).astype(BF16)

    for j in range(xbc_ref.shape[1] // PROJ_CHUNK):
        cs = slice(j * PROJ_CHUNK, (j + 1) * PROJ_CHUNK)
        acc = jnp.dot(h_ext, wc_ref[:, cs], preferred_element_type=F32)
        xbc_ref[:, cs] = conv(acc, HALO, tm, None)
        for b in inner_bounds:
            t = fix_row + (b - HALO)
            masks = [jnp.logical_or(in_lat, (t >= b) == (t + (k - mid) >= b)) for k in range(k_taps)]
            xbc_ref[b - HALO:b + HALO, cs] = conv(acc, b, 2 * HALO, masks)
    dr = jnp.dot(hm, wdt_ref[...], preferred_element_type=F32) + dtb_ref[...]
    dt_ref[...] = _softplus(dr)


def _gmlp_proj_kernel(x_ref, g_ref, sh_ref, sc_ref, w_ref, o_ref, *, n_chunks):
    hb = _norm_mod(x_ref[...], g_ref[...], sh_ref[...], sc_ref[...]).astype(BF16)
    for j in range(n_chunks):
        cs = slice(j * PROJ_CHUNK, (j + 1) * PROJ_CHUNK)
        acc = jnp.dot(hb, w_ref[:, cs], preferred_element_type=F32)
        o_ref[:, cs] = _gelu_tanh(acc).astype(BF16)


def _ret_proj_kernel(x_ref, g_ref, sh_ref, sc_ref, w_ref, cos_ref, sin_ref, o_ref, *, n_chunks, n_heads, k_scale):
    hb = _norm_mod(x_ref[...], g_ref[...], sh_ref[...], sc_ref[...]).astype(BF16)
    half = LANES // 2
    for j in range(n_chunks):
        cs = slice(j * PROJ_CHUNK, (j + 1) * PROJ_CHUNK)
        acc = jnp.dot(hb, w_ref[:, cs], preferred_element_type=F32)
        if j < 2 * n_heads:
            rolled = jnp.concatenate(
                [pltpu.roll(acc[:, :LANES], half, 1), pltpu.roll(acc[:, LANES:], half, 1)], axis=1)
            acc = acc * cos_ref[...] + rolled * sin_ref[...]
            if j >= n_heads:
                acc = acc * k_scale
        o_ref[:, cs] = acc.astype(BF16)


def _proj_common_specs(lay, tm, d, n_out):
    return [
        pl.BlockSpec((tm, d), lambda i: (i, 0)),
        _const_spec((1, d)),
        _mod_spec(lay, tm, 0, d),
        _mod_spec(lay, tm, 1, d),
        _const_spec((d, n_out)),
    ]


def _ssd_proj(lay, xall, g, modl, w_z, w_xbc, conv_w, conv_b, w_dt, dt_bias):
    t, d = xall.shape
    nz, nc = w_z.shape[1], w_xbc.shape[1]
    k_taps = conv_w.shape[0]
    assert k_taps // 2 <= HALO
    tm = _pick_tile((512, 256, 128), lay.seq, lay.t_ctx)
    assert tm % lay.ctx == 0 or lay.ctx % tm == 0
    hb = tm // HALO
    last_hb = t // HALO - 1
    return pl.pallas_call(
        functools.partial(_ssd_proj_kernel, tm=tm, lay=lay, k_taps=k_taps),
        grid=(t // tm,),
        in_specs=[
            pl.BlockSpec((tm, d), lambda i: (i, 0)),
            pl.BlockSpec((HALO, d), lambda i: (jnp.maximum(i * hb - 1, 0), 0)),
            pl.BlockSpec((HALO, d), lambda i: (jnp.minimum((i + 1) * hb, last_hb), 0)),
            _const_spec((1, d)), _mod_spec(lay, tm, 0, d), _mod_spec(lay, tm, 1, d),
            _const_spec((d, nz)), _const_spec((d, nc)), _const_spec((k_taps, nc)), _const_spec((1, nc)),
            _const_spec((d, LANES)), _const_spec((1, LANES)),
        ],
        out_specs=[pl.BlockSpec((tm, nz), lambda i: (i, 0)), pl.BlockSpec((tm, nc), lambda i: (i, 0)),
                   pl.BlockSpec((tm, LANES), lambda i: (i, 0))],
        out_shape=[jax.ShapeDtypeStruct((t, nz), BF16), jax.ShapeDtypeStruct((t, nc), BF16),
                   jax.ShapeDtypeStruct((t, LANES), F32)],
        compiler_params=_cparams("arbitrary"),
        name="ssd_proj",
    )(xall, xall, xall, g, modl, modl, w_z, w_xbc, conv_w, conv_b.reshape(1, nc), w_dt, dt_bias)


def _gmlp_proj(lay, xall, g, modl, w_in):
    t, d = xall.shape
    n_out = w_in.shape[1]
    tm = _pick_tile((512, 256, 128), lay.seq, lay.t_ctx)
    return pl.pallas_call(
        functools.partial(_gmlp_proj_kernel, n_chunks=n_out // PROJ_CHUNK),
        grid=(t // tm,),
        in_specs=_proj_common_specs(lay, tm, d, n_out),
        out_specs=pl.BlockSpec((tm, n_out), lambda i: (i, 0)),
        out_shape=jax.ShapeDtypeStruct((t, n_out), BF16),
        compiler_params=_cparams("arbitrary"),
        name="gmlp_proj",
    )(xall, g, modl, modl, w_in)


def _rope_tables(lay, tm, head_dim):
    quarter = head_dim // 4
    pos = jnp.arange(lay.seq)
    inv = ROPE_BASE ** (-jnp.arange(quarter, dtype=F32) / quarter)
    ang_r = (pos // GRID_W).astype(F32)[:, None] * inv
    ang_c = (pos % GRID_W).astype(F32)[:, None] * inv
    cos = jnp.concatenate([jnp.cos(ang_r)] * 2 + [jnp.cos(ang_c)] * 2, axis=1)
    sin = jnp.concatenate([-jnp.sin(ang_r), jnp.sin(ang_r), -jnp.sin(ang_c), jnp.sin(ang_c)], axis=1)
    cos = jnp.concatenate([cos, jnp.ones((tm, head_dim), F32)], axis=0)
    sin = jnp.concatenate([sin, jnp.zeros((tm, head_dim), F32)], axis=0)
    return cos, sin


def _ret_proj(lay, xall, g, modl, w_in):
    t, d = xall.shape
    n_out = w_in.shape[1]
    head_dim = d // RET_HEADS
    assert head_dim == PROJ_CHUNK
    tm = _pick_tile((512, 256, 128), lay.seq, lay.t_ctx)
    cos, sin = _rope_tables(lay, tm, head_dim)
    lat_tiles = lay.seq // tm

    def tab_idx(i):
        return (jnp.where(i * tm < lay.t_lat, i % lat_tiles, lat_tiles), 0)

    return pl.pallas_call(
        functools.partial(_ret_proj_kernel, n_chunks=n_out // PROJ_CHUNK, n_heads=RET_HEADS,
                          k_scale=head_dim ** -0.5),
        grid=(t // tm,),
        in_specs=_proj_common_specs(lay, tm, d, n_out) + [
            pl.BlockSpec((tm, head_dim), tab_idx), pl.BlockSpec((tm, head_dim), tab_idx)],
        out_specs=pl.BlockSpec((tm, n_out), lambda i: (i, 0)),
        out_shape=jax.ShapeDtypeStruct((t, n_out), BF16),
        compiler_params=_cparams("arbitrary"),
        name="ret_proj",
    )(xall, g, modl, modl, w_in, cos, sin)


def _tri_masks(q):
    ri = lax.broadcasted_iota(jnp.int32, (q, q), 0)
    ci = lax.broadcasted_iota(jnp.int32, (q, q), 1)
    return ri >= ci, ri <= ci


def _ssd_cums(dt, alog_ref):
    q = dt.shape[0]
    tril, triu = _tri_masks(q)
    a = dt * (-LOG2E * jnp.exp(alog_ref[...]))
    cum_f = jnp.dot(tril.astype(F32), a, preferred_element_type=F32, precision=HI)
    cum_b = jnp.dot(triu.astype(F32), a, preferred_element_type=F32, precision=HI)
    return cum_f, cum_b, tril, triu


def _ssd_bwd_kernel(x_ref, b_ref, c_ref, dt_ref, alog_ref, expb_ref, yb_ref, s_ref, *, groups):
    @pl.when(pl.program_id(1) == 0)
    def _():
        s_ref[...] = jnp.zeros_like(s_ref)

    q = dt_ref.shape[0]
    gw = x_ref.shape[1] // groups
    n = b_ref.shape[1] // groups
    dt = dt_ref[...]
    _, cum_b, _, _ = _ssd_cums(dt, alog_ref)
    expb = expb_ref[...]
    e_in = jnp.dot(jnp.exp2(cum_b).astype(BF16), expb, preferred_element_type=F32)
    w_in = jnp.dot((dt * jnp.exp2(cum_b[0:1, :] - cum_b)).astype(BF16), expb, preferred_element_type=F32)
    for g in range(groups):
        gs = slice(g * gw, (g + 1) * gw)
        ns = slice(g * n, (g + 1) * n)
        s_g = s_ref[:, gs]
        y = jnp.dot(c_ref[:, ns], s_g.astype(BF16), preferred_element_type=F32) * e_in[:, gs]
        yb_ref[:, gs] = y.astype(BF16)
        xw = (x_ref[:, gs].astype(F32) * w_in[:, gs]).astype(BF16)
        upd = lax.dot_general(b_ref[:, ns], xw, (((0,), (0,)), ((), ())), preferred_element_type=F32)
        s_ref[:, gs] = s_g * e_in[0:1, gs] + upd


def _ssd_fwd_kernel(x_ref, b_ref, c_ref, dt_ref, z_ref, yb_ref, alog_ref, expf_ref, expb_ref, dskip_ref, ng_ref,
                    o_ref, s_ref, y_ref, *, groups, hpg, p):
    @pl.when(pl.program_id(1) == 0)
    def _():
        s_ref[...] = jnp.zeros_like(s_ref)

    q = dt_ref.shape[0]
    inner = x_ref.shape[1]
    gw = inner // groups
    n = b_ref.shape[1] // groups
    heads = groups * hpg
    dt = dt_ref[...]
    cum_f, cum_b, tril, _ = _ssd_cums(dt, alog_ref)
    ldt = jnp.log(dt) * LOG2E
    row_f = (ldt - cum_f).T
    row_b = (ldt - cum_b).T
    expf = expf_ref[...]
    e_f = jnp.dot(jnp.exp2(cum_f).astype(BF16), expf, preferred_element_type=F32)
    w_f = jnp.dot((dt * jnp.exp2(cum_f[q - 1:q, :] - cum_f)).astype(BF16), expf, preferred_element_type=F32)
    lane = lax.broadcasted_iota(jnp.int32, (q, 2 * p), 1)
    left = lane < p
    eye = lax.broadcasted_iota(jnp.int32, (q, q), 0) == lax.broadcasted_iota(jnp.int32, (q, q), 1)
    ssq = jnp.zeros((q, 1), F32)
    for g in range(groups):
        gs = slice(g * gw, (g + 1) * gw)
        ns = slice(g * n, (g + 1) * n)
        c_g = c_ref[:, ns]
        b_g = b_ref[:, ns]
        scores = lax.dot_general(c_g, b_g, (((1,), (1,)), ((), ())), preferred_element_type=F32)
        s_g = s_ref[:, gs]
        pieces = []
        for pr in range(hpg // 2):
            ms = []
            for r in (g * hpg + 2 * pr, g * hpg + 2 * pr + 1):
                rb = heads + r
                arg = jnp.where(tril, cum_f[:, r:r + 1] + row_f[r:r + 1, :], cum_b[:, rb:rb + 1] + row_b[rb:rb + 1, :])
                ms.append((scores * jnp.exp2(arg)).astype(BF16))
            lhs = jnp.concatenate(ms, axis=1)
            ps = slice(g * gw + pr * 2 * p, g * gw + (pr + 1) * 2 * p)
            xp = x_ref[:, ps]
            zero = jnp.zeros_like(xp)
            rhs = jnp.concatenate([jnp.where(left, xp, zero), jnp.where(left, zero, xp)], axis=0)
            pieces.append(jnp.dot(lhs, rhs, preferred_element_type=F32))
        x_g = x_ref[:, gs].astype(F32)
        diag = jnp.sum(jnp.where(eye, scores, 0.0), axis=-1, keepdims=True)
        skip = dskip_ref[:, gs] + jnp.dot((diag * dt).astype(BF16), expb_ref[:, gs], preferred_element_type=F32)
        y = jnp.concatenate(pieces, axis=1)
        y = y + jnp.dot(c_g, s_g.astype(BF16), preferred_element_type=F32) * e_f[:, gs]
        y = y + yb_ref[:, gs].astype(F32) + skip * x_g
        y = y * _silu(z_ref[:, gs].astype(F32))
        ssq = ssq + jnp.sum(y * y, axis=-1, keepdims=True)
        y_ref[:, gs] = y
        xw = (x_g * w_f[:, gs]).astype(BF16)
        upd = lax.dot_general(b_g, xw, (((0,), (0,)), ((), ())), preferred_element_type=F32)
        s_ref[:, gs] = s_g * e_f[q - 1:q, gs] + upd
    rms = lax.rsqrt(ssq / inner + EPS)
    o_ref[...] = (y_ref[...] * rms * ng_ref[...]).astype(BF16)


def _ssd_scan(lay, zx, xbc, dt, alog, d_skip, norm_g):
    t = zx.shape[0]
    inner = d_skip.shape[0] * SSD_HEAD_DIM
    heads = d_skip.shape[0]
    hpg = heads // SSD_GROUPS
    gn = SSD_GROUPS * SSD_STATE
    assert 2 * heads <= LANES and inner % gn == 0 and hpg % 2 == 0
    nsteps = lay.ncc + lay.ncl
    xb, bb, cb = 0, inner // gn, inner // gn + 1

    lane_head = jnp.arange(LANES)[:, None]
    col_head = (jnp.arange(inner) // SSD_HEAD_DIM)[None, :]
    exp_f = (lane_head == col_head).astype(BF16)
    exp_b = (lane_head == col_head + heads).astype(BF16)
    dskip = jnp.repeat(d_skip, SSD_HEAD_DIM).reshape(1, inner)

    def specs(reverse):
        cm = lambda b, s: lay.chunk_block(b, s, reverse)
        return dict(
            x=pl.BlockSpec((CHUNK, inner), lambda b, s: (cm(b, s), xb)),
            b=pl.BlockSpec((CHUNK, gn), lambda b, s: (cm(b, s), bb)),
            c=pl.BlockSpec((CHUNK, gn), lambda b, s: (cm(b, s), cb)),
            dt=pl.BlockSpec((CHUNK, LANES), lambda b, s: (cm(b, s), 0)),
            row=pl.BlockSpec((CHUNK, inner), lambda b, s: (cm(b, s), 0)),
        )

    sb = specs(True)
    yb = pl.pallas_call(
        functools.partial(_ssd_bwd_kernel, groups=SSD_GROUPS),
        grid=(lay.nb, nsteps),
        in_specs=[sb["x"], sb["b"], sb["c"], sb["dt"], _const_spec((1, LANES)), _const_spec((LANES, inner))],
        out_specs=sb["row"],
        out_shape=jax.ShapeDtypeStruct((t, inner), BF16),
        scratch_shapes=[pltpu.VMEM((SSD_STATE, inner), F32)],
        compiler_params=_cparams("arbitrary", "arbitrary"),
        name="ssd_scan_bwd",
    )(xbc, xbc, xbc, dt, alog, exp_b)

    sf = specs(False)
    return pl.pallas_call(
        functools.partial(_ssd_fwd_kernel, groups=SSD_GROUPS, hpg=hpg, p=SSD_HEAD_DIM),
        grid=(lay.nb, nsteps),
        in_specs=[sf["x"], sf["b"], sf["c"], sf["dt"], sf["row"], sf["row"], _const_spec((1, LANES)),
                  _const_spec((LANES, inner)), _const_spec((LANES, inner)), _const_spec((1, inner)),
                  _const_spec((1, inner))],
        out_specs=sf["row"],
        out_shape=jax.ShapeDtypeStruct((t, inner), BF16),
        scratch_shapes=[pltpu.VMEM((SSD_STATE, inner), F32), pltpu.VMEM((CHUNK, inner), F32)],
        compiler_params=_cparams("arbitrary", "arbitrary"),
        name="ssd_scan_fwd",
    )(xbc, xbc, xbc, dt, zx, yb, alog, exp_f, exp_b, dskip, norm_g.reshape(1, inner))


def _gmlp_gate_kernel(u_ref, v_ref, ng_ref, ws_ref, bias_ref, o_ref, *, groups):
    tm, width = u_ref.shape
    gd = width // groups
    v = v_ref[...].astype(F32)
    ms = jnp.mean(v * v, axis=-1, keepdims=True)
    vn = (v * lax.rsqrt(ms + EPS) * ng_ref[...]).astype(BF16)
    for c in range(tm // CHUNK):
        rs = slice(c * CHUNK, (c + 1) * CHUNK)
        for g in range(groups):
            gs = slice(g * gd, (g + 1) * gd)
            mixed = jnp.dot(ws_ref[g], vn[rs, gs], preferred_element_type=F32) + bias_ref[:, gs]
            o_ref[rs, gs] = (u_ref[rs, gs].astype(F32) * mixed).astype(BF16)


def _gmlp_gate(lay, uv, norm_g, w_s, b_s):
    t = uv.shape[0]
    width = uv.shape[1] // 2
    groups = w_s.shape[0]
    tm = _pick_tile((512, 256, 128), lay.seq, lay.t_ctx)
    bias = jnp.repeat(b_s.T, width // groups, axis=1)
    return pl.pallas_call(
        functools.partial(_gmlp_gate_kernel, groups=groups),
        grid=(t // tm,),
        in_specs=[
            pl.BlockSpec((tm, width), lambda i: (i, 0)),
            pl.BlockSpec((tm, width), lambda i: (i, 1)),
            _const_spec((1, width)),
            _const_spec((groups, CHUNK, CHUNK)),
            _const_spec((CHUNK, width)),
        ],
        out_specs=pl.BlockSpec((tm, width), lambda i: (i, 0)),
        out_shape=jax.ShapeDtypeStruct((t, width), BF16),
        compiler_params=_cparams("arbitrary"),
        name="gmlp_gate",
    )(uv, uv, norm_g.reshape(1, width), w_s.astype(BF16), bias)


def _ret_decay_tiles(dec_ref, row, h, q):
    return -jnp.exp(jnp.full((q, q), dec_ref[row, h], F32))


def _ret_bwd_kernel(dec_ref, q_ref, k_ref, v_ref, yb_ref, s_ref, *, heads):
    @pl.when(pl.program_id(1) == 0)
    def _():
        s_ref[...] = jnp.zeros_like(s_ref)

    q = q_ref.shape[0]
    dk = q_ref.shape[1] // heads
    dv = v_ref.shape[1] // heads
    pos = lax.broadcasted_iota(jnp.int32, (q, q), 0).astype(F32)
    for h in range(heads):
        lg = _ret_decay_tiles(dec_ref, 1, h, q)
        e_in = jnp.exp(lg * (q - pos))
        tail = jnp.exp(lg * pos)
        s_h = s_ref[h]
        ks = slice(h * dk, (h + 1) * dk)
        vs = slice(h * dv, (h + 1) * dv)
        y = jnp.dot(q_ref[:, ks], s_h.astype(BF16), preferred_element_type=F32)
        yb_ref[:, vs] = (y * jnp.concatenate([e_in] * (dv // q), axis=1)).astype(BF16)
        kt = (k_ref[:, ks].astype(F32) * jnp.concatenate([tail] * (dk // q), axis=1)).astype(BF16)
        upd = lax.dot_general(kt, v_ref[:, vs], (((0,), (0,)), ((), ())), preferred_element_type=F32)
        s_ref[h] = s_h * jnp.concatenate([jnp.exp(lg[0:1, :] * q)] * (dv // q), axis=1) + upd


def _ret_fwd_kernel(dec_ref, q_ref, k_ref, v_ref, g_ref, yb_ref, o_ref, s_ref, *, heads):
    @pl.when(pl.program_id(1) == 0)
    def _():
        s_ref[...] = jnp.zeros_like(s_ref)

    q = q_ref.shape[0]
    dk = q_ref.shape[1] // heads
    dv = v_ref.shape[1] // heads
    pos = lax.broadcasted_iota(jnp.int32, (q, q), 0).astype(F32)
    diff = pos - lax.broadcasted_iota(jnp.int32, (q, q), 1).astype(F32)
    for h in range(heads):
        lg_f = _ret_decay_tiles(dec_ref, 0, h, q)
        lg_b = _ret_decay_tiles(dec_ref, 1, h, q)
        dec = (jnp.where(diff >= 0, jnp.exp(lg_f * jnp.maximum(diff, 0.0)), 0.0)
               + jnp.where(diff <= 0, jnp.exp(lg_b * jnp.maximum(-diff, 0.0)), 0.0))
        e_in = jnp.exp(lg_f * (pos + 1.0))
        tail = jnp.exp(lg_f * (q - 1.0 - pos))
        s_h = s_ref[h]
        ks = slice(h * dk, (h + 1) * dk)
        vs = slice(h * dv, (h + 1) * dv)
        q_h, k_h, v_h = q_ref[:, ks], k_ref[:, ks], v_ref[:, vs]
        scores = lax.dot_general(q_h, k_h, (((1,), (1,)), ((), ())), preferred_element_type=F32)
        y = jnp.dot((scores * dec).astype(BF16), v_h, preferred_element_type=F32)
        y = y + jnp.dot(q_h, s_h.astype(BF16), preferred_element_type=F32) * jnp.concatenate(
            [e_in] * (dv // q), axis=1)
        y = y + yb_ref[:, vs].astype(F32)
        y = y * lax.rsqrt(jnp.mean(y * y, axis=-1, keepdims=True) + EPS)
        o_ref[:, vs] = (_silu(g_ref[:, vs].astype(F32)) * y).astype(BF16)
        kt = (k_h.astype(F32) * jnp.concatenate([tail] * (dk // q), axis=1)).astype(BF16)
        upd = lax.dot_general(kt, v_h, (((0,), (0,)), ((), ())), preferred_element_type=F32)
        s_ref[h] = s_h * jnp.concatenate([jnp.exp(lg_f[0:1, :] * q)] * (dv // q), axis=1) + upd


def _ret_scan(lay, qkvg, decay_f, decay_b, d):
    t = qkvg.shape[0]
    heads = decay_f.shape[0]
    dv = 2 * d
    nsteps = lay.ncc + lay.ncl
    dec = jnp.stack([decay_f, decay_b]).astype(F32)
    smem = pl.BlockSpec(memory_space=pltpu.SMEM)
    state = pltpu.VMEM((heads, d // heads, dv // heads), F32)

    def specs(reverse):
        cm = lambda b, s: lay.chunk_block(b, s, reverse)
        return dict(
            q=pl.BlockSpec((CHUNK, d), lambda b, s: (cm(b, s), 0)),
            k=pl.BlockSpec((CHUNK, d), lambda b, s: (cm(b, s), 1)),
            v=pl.BlockSpec((CHUNK, dv), lambda b, s: (cm(b, s), 1)),
            g=pl.BlockSpec((CHUNK, dv), lambda b, s: (cm(b, s), 2)),
            row=pl.BlockSpec((CHUNK, dv), lambda b, s: (cm(b, s), 0)),
        )

    sb = specs(True)
    yb = pl.pallas_call(
        functools.partial(_ret_bwd_kernel, heads=heads),
        grid=(lay.nb, nsteps),
        in_specs=[smem, sb["q"], sb["k"], sb["v"]],
        out_specs=sb["row"],
        out_shape=jax.ShapeDtypeStruct((t, dv), BF16),
        scratch_shapes=[state],
        compiler_params=_cparams("arbitrary", "arbitrary"),
        name="ret_scan_bwd",
    )(dec, qkvg, qkvg, qkvg)

    sf = specs(False)
    return pl.pallas_call(
        functools.partial(_ret_fwd_kernel, heads=heads),
        grid=(lay.nb, nsteps),
        in_specs=[smem, sf["q"], sf["k"], sf["v"], sf["g"], sf["row"]],
        out_specs=sf["row"],
        out_shape=jax.ShapeDtypeStruct((t, dv), BF16),
        scratch_shapes=[state],
        compiler_params=_cparams("arbitrary", "arbitrary"),
        name="ret_scan_fwd",
    )(dec, qkvg, qkvg, qkvg, qkvg, yb)


def _ffn_kernel(x_ref, y_ref, g1_ref, sh_ref, sc_ref, g2_ref, ng_ref, wo_ref, w1_ref, w3_ref, w2_ref, fg_ref,
                o_ref, acc_ref, *, n_ff, final_norm):
    x1 = x_ref[...] + g1_ref[...] * jnp.dot(y_ref[...], wo_ref[...], preferred_element_type=F32)
    hb = _norm_mod(x1, ng_ref[...], sh_ref[...], sc_ref[...]).astype(BF16)
    for f in range(n_ff):
        fs = slice(f * FF_CHUNK, (f + 1) * FF_CHUNK)
        a = jnp.dot(hb, w1_ref[:, fs], preferred_element_type=F32)
        b = jnp.dot(hb, w3_ref[:, fs], preferred_element_type=F32)
        part = jnp.dot((_silu(a) * b).astype(BF16), w2_ref[fs, :], preferred_element_type=F32)
        if f == 0:
            acc_ref[...] = part
        else:
            acc_ref[...] += part
    out = x1 + g2_ref[...] * acc_ref[...]
    if final_norm:
        ms = jnp.mean(out * out, axis=-1, keepdims=True)
        out = out * lax.rsqrt(ms + EPS) * fg_ref[...]
    o_ref[...] = out


def _ffn(lay, xall, y, modl, norm2_g, w_out, w1, w3, w2, final_g, *, last):
    d = xall.shape[1]
    dy = y.shape[1]
    dff = w1.shape[1]
    assert dff % FF_CHUNK == 0
    tm = _pick_tile((512, 256, 128), lay.seq, lay.t_ctx)
    t_out = lay.t_lat if last else lay.t_all
    return pl.pallas_call(
        functools.partial(_ffn_kernel, n_ff=dff // FF_CHUNK, final_norm=last),
        grid=(t_out // tm,),
        in_specs=[
            pl.BlockSpec((tm, d), lambda i: (i, 0)),
            pl.BlockSpec((tm, dy), lambda i: (i, 0)),
            _mod_spec(lay, tm, 2, d), _mod_spec(lay, tm, 3, d), _mod_spec(lay, tm, 4, d), _mod_spec(lay, tm, 5, d),
            _const_spec((1, d)),
            _const_spec((dy, d)), _const_spec((d, dff)), _const_spec((d, dff)), _const_spec((dff, d)),
            _const_spec((1, d)),
        ],
        out_specs=pl.BlockSpec((tm, d), lambda i: (i, 0)),
        out_shape=jax.ShapeDtypeStruct((t_out, d), F32),
        scratch_shapes=[pltpu.VMEM((tm, d), F32)],
        compiler_params=_cparams("arbitrary"),
        name="ffn",
    )(xall, y, modl, modl, modl, modl, norm2_g, w_out, w1, w3, w2, final_g)


def kernel(x, c, ctx, c_ctx, mod_w, mod_b, norm1_g, norm2_g, ffn_w1, ffn_w3, ffn_w2, ssd_w_in, ssd_conv_w, ssd_conv_b, ssd_a_log_f, ssd_a_log_b, ssd_dt_bias_f, ssd_dt_bias_b, ssd_d, ssd_norm_g, ssd_w_out, gmlp_w_in, gmlp_norm_g, gmlp_w_s, gmlp_b_s, gmlp_w_out, ret_w_in, ret_decay_f, ret_decay_b, ret_w_out, final_g):
    nb, seq, d = x.shape
    ctx_len = ctx.shape[1]
    depth = mod_w.shape[0]
    assert seq % CHUNK == 0 and ctx_len % CHUNK == 0 and seq % GRID_W == 0
    lay = _Layout(nb, seq, ctx_len)

    rows = -(-(nb + 1) // 8) * 8
    cvec = jnp.zeros((rows, d), F32).at[:nb].set(c).at[nb].set(c_ctx)
    mod = _modulation(cvec, mod_w, mod_b)

    xall = jnp.concatenate([x.reshape(nb * seq, d), ctx.reshape(nb * ctx_len, d)], axis=0)
    final_g2 = final_g.reshape(1, d)

    for i in range(depth):
        last = i == depth - 1
        kind, j = i % N_MIXERS, i // N_MIXERS
        modl = mod[i].reshape(rows * 6, 1, d)
        g1 = norm1_g[i].reshape(1, d)
        if kind == 0:
            heads = ssd_d.shape[1]
            inner = heads * SSD_HEAD_DIM
            n_main = 2 * inner + 2 * SSD_GROUPS * SSD_STATE
            w_in = ssd_w_in[j]
            w_dt = jnp.zeros((d, LANES), F32).at[:, :2 * heads].set(w_in[:, n_main:]).astype(BF16)
            dt_bias = jnp.zeros((1, LANES), F32).at[0, :2 * heads].set(
                jnp.concatenate([ssd_dt_bias_f[j], ssd_dt_bias_b[j]]))
            alog = jnp.zeros((1, LANES), F32).at[0, :2 * heads].set(
                jnp.concatenate([ssd_a_log_f[j], ssd_a_log_b[j]]))
            zx, xbc, dt = _ssd_proj(lay, xall, g1, modl, w_in[:, :inner].astype(BF16),
                                    w_in[:, inner:n_main].astype(BF16), ssd_conv_w[j], ssd_conv_b[j], w_dt, dt_bias)
            y = _ssd_scan(lay, zx, xbc, dt, alog, ssd_d[j], ssd_norm_g[j])
            w_out = ssd_w_out[j]
        elif kind == 1:
            uv = _gmlp_proj(lay, xall, g1, modl, gmlp_w_in[j].astype(BF16))
            y = _gmlp_gate(lay, uv, gmlp_norm_g[j], gmlp_w_s[j], gmlp_b_s[j])
            w_out = gmlp_w_out[j]
        else:
            qkvg = _ret_proj(lay, xall, g1, modl, ret_w_in[j].astype(BF16))
            y = _ret_scan(lay, qkvg, ret_decay_f[j], ret_decay_b[j], d)
            w_out = ret_w_out[j]
        xall = _ffn(lay, xall, y, modl, norm2_g[i].reshape(1, d), w_out.astype(BF16),
                    ffn_w1[i].astype(BF16), ffn_w3[i].astype(BF16), ffn_w2[i].astype(BF16), final_g2, last=last)
    return xall.reshape(nb, seq, d)
```

```python
import functools

import jax
import jax.numpy as jnp
from jax import lax
from jax.experimental import pallas as pl
from jax.experimental.pallas import tpu as pltpu

F32 = jnp.float32
BF16 = jnp.bfloat16
HI = lax.Precision.HIGHEST

EPS = 1e-6
CHUNK = 128
GRID_W = 64
ROPE_BASE = 10000.0
N_MIXERS = 3
SSD_HEAD_DIM = 64
SSD_GROUPS = 4
SSD_STATE = 128
SSD_CONV = 5
GMLP_GROUPS = 8
RET_HEADS = 4
LANES = 128
HALO = 16
FF_CHUNK = 256
PROJ_CHUNK = 256
VMEM_LIMIT = 56 * 1024 * 1024
LOG2E = 1.4426950408889634


def _cparams(*sem):
    return pltpu.CompilerParams(dimension_semantics=sem, vmem_limit_bytes=VMEM_LIMIT)


def _pick_tile(cands, *dims):
    for t in cands:
        if all(d % t == 0 for d in dims):
            return t
    raise ValueError(f"no tile in {cands} divides {dims}")


def _silu(x):
    return x / (1.0 + jnp.exp(-x))


def _gelu_tanh(x):
    return 0.5 * x * (1.0 + jnp.tanh(0.7978845608028654 * (x + 0.044715 * (x * x * x))))


def _softplus(x):
    return jnp.maximum(x, 0.0) + jnp.log(1.0 + jnp.exp(-jnp.abs(x)))


def _norm_mod(x, g, sh, sc):
    ms = jnp.mean(x * x, axis=-1, keepdims=True)
    y = x * lax.rsqrt(ms + EPS) * g
    return y * (1.0 + sc) + sh


def _const_spec(shape):
    nd = len(shape)
    return pl.BlockSpec(shape, lambda *_: (0,) * nd, pipeline_mode=pl.Buffered(1))


def _mod_kernel(c_ref, w_ref, b_ref, o_ref):
    a = _silu(c_ref[...])
    o_ref[...] = jnp.dot(a, w_ref[...], preferred_element_type=F32, precision=HI) + b_ref[...]


def _modulation(cvec, mod_w, mod_b):
    depth, d, d6 = mod_w.shape
    r = cvec.shape[0]
    nk = d6 // d
    return pl.pallas_call(
        _mod_kernel,
        grid=(depth, nk),
        in_specs=[
            pl.BlockSpec((r, d), lambda l, k: (0, 0)),
            pl.BlockSpec((None, d, d), lambda l, k: (l, 0, k)),
            pl.BlockSpec((None, 1, d), lambda l, k: (l, 0, k)),
        ],
        out_specs=pl.BlockSpec((None, r, d), lambda l, k: (l, 0, k)),
        out_shape=jax.ShapeDtypeStruct((depth, r, d6), F32),
        compiler_params=_cparams("arbitrary", "arbitrary"),
        name="modulation",
    )(cvec, mod_w, mod_b.reshape(depth, 1, d6))


class _Layout:
    def __init__(self, nb, seq, ctx):
        self.nb, self.seq, self.ctx = nb, seq, ctx
        self.t_lat = nb * seq
        self.t_ctx = nb * ctx
        self.t_all = self.t_lat + self.t_ctx
        self.ncl = seq // CHUNK
        self.ncc = ctx // CHUNK
        self.scan_sub = _pick_tile((2, 1), self.ncc, self.ncl)
        self.scan_rows = self.scan_sub * CHUNK
        self.scan_steps = (self.ncc + self.ncl) // self.scan_sub

    def mod_row(self, i, tm):
        return jnp.minimum((i * tm) // self.seq, self.nb)

    def scan_block(self, b, s, reverse):
        ncc, ncl = self.ncc // self.scan_sub, self.ncl // self.scan_sub
        if reverse:
            c_ctx, c_lat = ncc - 1 - s, ncl - 1 - (s - ncc)
        else:
            c_ctx, c_lat = s, s - ncc
        return jnp.where(s < ncc, self.nb * ncl + b * ncc + c_ctx, b * ncl + c_lat)


def _for_each_chunk(refs, reverse, fn):
    nsub = refs[0].shape[0] // CHUNK
    for k in range(nsub):
        c = nsub - 1 - k if reverse else k
        fn(*[r.at[pl.ds(c * CHUNK, CHUNK)] for r in refs])


def _mod_spec(lay, tm, k, d):
    return pl.BlockSpec((None, 1, d), lambda i, *_: (lay.mod_row(i, tm) * 6 + k, 0, 0))


def _ssd_proj_kernel(x_ref, xp_ref, xn_ref, g_ref, sh_ref, sc_ref, wz_ref, wc_ref, cw_ref, cb_ref, wdt_ref, dtb_ref,
                     z_ref, xbc_ref, dt_ref, *, tm, lay, k_taps):
    i = pl.program_id(0)
    t0 = i * tm
    in_lat = t0 < lay.t_lat
    off = jnp.where(in_lat, t0, t0 - lay.t_lat)
    length = jnp.where(in_lat, lay.seq, lay.ctx)
    at_start = (off % length) == 0
    at_end = ((off + tm) % length) == 0
    g, sh, sc = g_ref[...], sh_ref[...], sc_ref[...]
    hm = _norm_mod(x_ref[...], g, sh, sc).astype(BF16)
    hp = jnp.where(at_start, 0.0, _norm_mod(xp_ref[...], g, sh, sc)).astype(BF16)
    hn = jnp.where(at_end, 0.0, _norm_mod(xn_ref[...], g, sh, sc)).astype(BF16)
    h_ext = jnp.concatenate([hp, hm, hn], axis=0)
    for j in range(z_ref.shape[1] // PROJ_CHUNK):
        cs = slice(j * PROJ_CHUNK, (j + 1) * PROJ_CHUNK)
        z_ref[:, cs] = jnp.dot(hm, wz_ref[:, cs], preferred_element_type=F32).astype(BF16)
    mid = k_taps // 2
    inner_bounds = list(range(lay.ctx, tm, lay.ctx))
    fix_row = lax.broadcasted_iota(jnp.int32, (2 * HALO, 1), 0)

    def conv(acc, row0, rows, masks):
        y = cb_ref[:, cs]
        for k in range(k_taps):
            term = cw_ref[k:k + 1, cs] * acc[row0 + k - mid:row0 + k - mid + rows]
            y = y + (term if masks is None else jnp.where(masks[k], term, 0.0))
        return _silu(y).astype(BF16)

    for j in range(xbc_ref.shape[1] // PROJ_CHUNK):
        cs = slice(j * PROJ_CHUNK, (j + 1) * PROJ_CHUNK)
        acc = jnp.dot(h_ext, wc_ref[:, cs], preferred_element_type=F32)
        xbc_ref[:, cs] = conv(acc, HALO, tm, None)
        for b in inner_bounds:
            t = fix_row + (b - HALO)
            masks = [jnp.logical_or(in_lat, (t >= b) == (t + (k - mid) >= b)) for k in range(k_taps)]
            xbc_ref[b - HALO:b + HALO, cs] = conv(acc, b, 2 * HALO, masks)
    dr = jnp.dot(hm, wdt_ref[...], preferred_element_type=F32) + dtb_ref[...]
    dt_ref[...] = _softplus(dr)


def _gmlp_proj_kernel(x_ref, g_ref, sh_ref, sc_ref, w_ref, o_ref, *, n_chunks):
    hb = _norm_mod(x_ref[...], g_ref[...], sh_ref[...], sc_ref[...]).astype(BF16)
    for j in range(n_chunks):
        cs = slice(j * PROJ_CHUNK, (j + 1) * PROJ_CHUNK)
        acc = jnp.dot(hb, w_ref[:, cs], preferred_element_type=F32)
        o_ref[:, cs] = _gelu_tanh(acc).astype(BF16)


def _ret_proj_kernel(x_ref, g_ref, sh_ref, sc_ref, w_ref, cos_ref, sin_ref, o_ref, *, n_chunks, n_heads, k_scale):
    hb = _norm_mod(x_ref[...], g_ref[...], sh_ref[...], sc_ref[...]).astype(BF16)
    half = LANES // 2
    for j in range(n_chunks):
        cs = slice(j * PROJ_CHUNK, (j + 1) * PROJ_CHUNK)
        acc = jnp.dot(hb, w_ref[:, cs], preferred_element_type=F32)
        if j < 2 * n_heads:
            rolled = jnp.concatenate(
                [pltpu.roll(acc[:, :LANES], half, 1), pltpu.roll(acc[:, LANES:], half, 1)], axis=1)
            acc = acc * cos_ref[...] + rolled * sin_ref[...]
            if j >= n_heads:
                acc = acc * k_scale
        o_ref[:, cs] = acc.astype(BF16)


def _proj_common_specs(lay, tm, d, n_out):
    return [
        pl.BlockSpec((tm, d), lambda i: (i, 0)),
        _const_spec((1, d)),
        _mod_spec(lay, tm, 0, d),
        _mod_spec(lay, tm, 1, d),
        _const_spec((d, n_out)),
    ]


def _ssd_proj(lay, xall, g, modl, w_z, w_xbc, conv_w, conv_b, w_dt, dt_bias):
    t, d = xall.shape
    nz, nc = w_z.shape[1], w_xbc.shape[1]
    k_taps = conv_w.shape[0]
    assert k_taps // 2 <= HALO
    tm = _pick_tile((512, 256, 128), lay.seq, lay.t_ctx)
    assert tm % lay.ctx == 0 or lay.ctx % tm == 0
    hb = tm // HALO
    last_hb = t // HALO - 1
    return pl.pallas_call(
        functools.partial(_ssd_proj_kernel, tm=tm, lay=lay, k_taps=k_taps),
        grid=(t // tm,),
        in_specs=[
            pl.BlockSpec((tm, d), lambda i: (i, 0)),
            pl.BlockSpec((HALO, d), lambda i: (jnp.maximum(i * hb - 1, 0), 0)),
            pl.BlockSpec((HALO, d), lambda i: (jnp.minimum((i + 1) * hb, last_hb), 0)),
            _const_spec((1, d)), _mod_spec(lay, tm, 0, d), _mod_spec(lay, tm, 1, d),
            _const_spec((d, nz)), _const_spec((d, nc)), _const_spec((k_taps, nc)), _const_spec((1, nc)),
            _const_spec((d, LANES)), _const_spec((1, LANES)),
        ],
        out_specs=[pl.BlockSpec((tm, nz), lambda i: (i, 0)), pl.BlockSpec((tm, nc), lambda i: (i, 0)),
                   pl.BlockSpec((tm, LANES), lambda i: (i, 0))],
        out_shape=[jax.ShapeDtypeStruct((t, nz), BF16), jax.ShapeDtypeStruct((t, nc), BF16),
                   jax.ShapeDtypeStruct((t, LANES), F32)],
        compiler_params=_cparams("arbitrary"),
        name="ssd_proj",
    )(xall, xall, xall, g, modl, modl, w_z, w_xbc, conv_w, conv_b.reshape(1, nc), w_dt, dt_bias)


def _gmlp_proj(lay, xall, g, modl, w_in):
    t, d = xall.shape
    n_out = w_in.shape[1]
    tm = _pick_tile((512, 256, 128), lay.seq, lay.t_ctx)
    return pl.pallas_call(
        functools.partial(_gmlp_proj_kernel, n_chunks=n_out // PROJ_CHUNK),
        grid=(t // tm,),
        in_specs=_proj_common_specs(lay, tm, d, n_out),
        out_specs=pl.BlockSpec((tm, n_out), lambda i: (i, 0)),
        out_shape=jax.ShapeDtypeStruct((t, n_out), BF16),
        compiler_params=_cparams("arbitrary"),
        name="gmlp_proj",
    )(xall, g, modl, modl, w_in)


def _rope_tables(lay, tm, head_dim):
    quarter = head_dim // 4
    pos = jnp.arange(lay.seq)
    inv = ROPE_BASE ** (-jnp.arange(quarter, dtype=F32) / quarter)
    ang_r = (pos // GRID_W).astype(F32)[:, None] * inv
    ang_c = (pos % GRID_W).astype(F32)[:, None] * inv
    cos = jnp.concatenate([jnp.cos(ang_r)] * 2 + [jnp.cos(ang_c)] * 2, axis=1)
    sin = jnp.concatenate([-jnp.sin(ang_r), jnp.sin(ang_r), -jnp.sin(ang_c), jnp.sin(ang_c)], axis=1)
    cos = jnp.concatenate([cos, jnp.ones((tm, head_dim), F32)], axis=0)
    sin = jnp.concatenate([sin, jnp.zeros((tm, head_dim), F32)], axis=0)
    return cos, sin


def _ret_proj(lay, xall, g, modl, w_in):
    t, d = xall.shape
    n_out = w_in.shape[1]
    head_dim = d // RET_HEADS
    assert head_dim == PROJ_CHUNK
    tm = _pick_tile((512, 256, 128), lay.seq, lay.t_ctx)
    cos, sin = _rope_tables(lay, tm, head_dim)
    lat_tiles = lay.seq // tm

    def tab_idx(i):
        return (jnp.where(i * tm < lay.t_lat, i % lat_tiles, lat_tiles), 0)

    return pl.pallas_call(
        functools.partial(_ret_proj_kernel, n_chunks=n_out // PROJ_CHUNK, n_heads=RET_HEADS,
                          k_scale=head_dim ** -0.5),
        grid=(t // tm,),
        in_specs=_proj_common_specs(lay, tm, d, n_out) + [
            pl.BlockSpec((tm, head_dim), tab_idx), pl.BlockSpec((tm, head_dim), tab_idx)],
        out_specs=pl.BlockSpec((tm, n_out), lambda i: (i, 0)),
        out_shape=jax.ShapeDtypeStruct((t, n_out), BF16),
        compiler_params=_cparams("arbitrary"),
        name="ret_proj",
    )(xall, g, modl, modl, w_in, cos, sin)


def _tri_masks(q):
    ri = lax.broadcasted_iota(jnp.int32, (q, q), 0)
    ci = lax.broadcasted_iota(jnp.int32, (q, q), 1)
    return ri >= ci, ri <= ci


def _ssd_cums(dt, alog_ref):
    q = dt.shape[0]
    tril, triu = _tri_masks(q)
    a = dt * (-LOG2E * jnp.exp(alog_ref[...]))
    cum_f = jnp.dot(tril.astype(F32), a, preferred_element_type=F32, precision=HI)
    cum_b = jnp.dot(triu.astype(F32), a, preferred_element_type=F32, precision=HI)
    return cum_f, cum_b, tril, triu


def _ssd_bwd_kernel(x_ref, b_ref, c_ref, dt_ref, alog_ref, expb_ref, yb_ref, s_ref, *, groups):
    @pl.when(pl.program_id(1) == 0)
    def _():
        s_ref[...] = jnp.zeros_like(s_ref)

    _for_each_chunk([x_ref, b_ref, c_ref, dt_ref, yb_ref], True, functools.partial(
        _ssd_bwd_chunk, alog_ref=alog_ref, expb_ref=expb_ref, s_ref=s_ref, groups=groups))


def _ssd_bwd_chunk(x_ref, b_ref, c_ref, dt_ref, yb_ref, *, alog_ref, expb_ref, s_ref, groups):
    gw = x_ref.shape[1] // groups
    n = b_ref.shape[1] // groups
    dt = dt_ref[...]
    _, cum_b, _, _ = _ssd_cums(dt, alog_ref)
    expb = expb_ref[...]
    e_in = jnp.dot(jnp.exp2(cum_b).astype(BF16), expb, preferred_element_type=F32)
    w_in = jnp.dot((dt * jnp.exp2(cum_b[0:1, :] - cum_b)).astype(BF16), expb, preferred_element_type=F32)
    for g in range(groups):
        gs = slice(g * gw, (g + 1) * gw)
        ns = slice(g * n, (g + 1) * n)
        s_g = s_ref[:, gs]
        y = jnp.dot(c_ref[:, ns], s_g.astype(BF16), preferred_element_type=F32) * e_in[:, gs]
        yb_ref[:, gs] = y.astype(BF16)
        xw = (x_ref[:, gs].astype(F32) * w_in[:, gs]).astype(BF16)
        upd = lax.dot_general(b_ref[:, ns], xw, (((0,), (0,)), ((), ())), preferred_element_type=F32)
        s_ref[:, gs] = s_g * e_in[0:1, gs] + upd


def _ssd_fwd_kernel(x_ref, b_ref, c_ref, dt_ref, z_ref, yb_ref, alog_ref, expf_ref, expb_ref, dskip_ref, ng_ref,
                    o_ref, s_ref, y_ref, *, groups, hpg, p):
    @pl.when(pl.program_id(1) == 0)
    def _():
        s_ref[...] = jnp.zeros_like(s_ref)

    _for_each_chunk([x_ref, b_ref, c_ref, dt_ref, z_ref, yb_ref, o_ref], False, functools.partial(
        _ssd_fwd_chunk, alog_ref=alog_ref, expf_ref=expf_ref, expb_ref=expb_ref, dskip_ref=dskip_ref,
        ng_ref=ng_ref, s_ref=s_ref, y_ref=y_ref, groups=groups, hpg=hpg, p=p))


def _ssd_fwd_chunk(x_ref, b_ref, c_ref, dt_ref, z_ref, yb_ref, o_ref, *, alog_ref, expf_ref, expb_ref, dskip_ref,
                   ng_ref, s_ref, y_ref, groups, hpg, p):
    q = dt_ref.shape[0]
    inner = x_ref.shape[1]
    gw = inner // groups
    n = b_ref.shape[1] // groups
    heads = groups * hpg
    dt = dt_ref[...]
    cum_f, cum_b, tril, _ = _ssd_cums(dt, alog_ref)
    ldt = jnp.log(dt) * LOG2E
    row_f = (ldt - cum_f).T
    row_b = (ldt - cum_b).T
    expf = expf_ref[...]
    e_f = jnp.dot(jnp.exp2(cum_f).astype(BF16), expf, preferred_element_type=F32)
    w_f = jnp.dot((dt * jnp.exp2(cum_f[q - 1:q, :] - cum_f)).astype(BF16), expf, preferred_element_type=F32)
    lane = lax.broadcasted_iota(jnp.int32, (q, 2 * p), 1)
    left = lane < p
    eye = lax.broadcasted_iota(jnp.int32, (q, q), 0) == lax.broadcasted_iota(jnp.int32, (q, q), 1)
    ssq = jnp.zeros((q, 1), F32)
    for g in range(groups):
        gs = slice(g * gw, (g + 1) * gw)
        ns = slice(g * n, (g + 1) * n)
        c_g = c_ref[:, ns]
        b_g = b_ref[:, ns]
        scores = lax.dot_general(c_g, b_g, (((1,), (1,)), ((), ())), preferred_element_type=F32)
        s_g = s_ref[:, gs]
        pieces = []
        for pr in range(hpg // 2):
            ms = []
            for r in (g * hpg + 2 * pr, g * hpg + 2 * pr + 1):
                rb = heads + r
                arg = jnp.where(tril, cum_f[:, r:r + 1] + row_f[r:r + 1, :], cum_b[:, rb:rb + 1] + row_b[rb:rb + 1, :])
                ms.append((scores * jnp.exp2(arg)).astype(BF16))
            lhs = jnp.concatenate(ms, axis=1)
            ps = slice(g * gw + pr * 2 * p, g * gw + (pr + 1) * 2 * p)
            xp = x_ref[:, ps]
            zero = jnp.zeros_like(xp)
            rhs = jnp.concatenate([jnp.where(left, xp, zero), jnp.where(left, zero, xp)], axis=0)
            pieces.append(jnp.dot(lhs, rhs, preferred_element_type=F32))
        x_g = x_ref[:, gs].astype(F32)
        diag = jnp.sum(jnp.where(eye, scores, 0.0), axis=-1, keepdims=True)
        skip = dskip_ref[:, gs] + jnp.dot((diag * dt).astype(BF16), expb_ref[:, gs], preferred_element_type=F32)
        y = jnp.concatenate(pieces, axis=1)
        y = y + jnp.dot(c_g, s_g.astype(BF16), preferred_element_type=F32) * e_f[:, gs]
        y = y + yb_ref[:, gs].astype(F32) + skip * x_g
        y = y * _silu(z_ref[:, gs].astype(F32))
        ssq = ssq + jnp.sum(y * y, axis=-1, keepdims=True)
        y_ref[:, gs] = y
        xw = (x_g * w_f[:, gs]).astype(BF16)
        upd = lax.dot_general(b_g, xw, (((0,), (0,)), ((), ())), preferred_element_type=F32)
        s_ref[:, gs] = s_g * e_f[q - 1:q, gs] + upd
    rms = lax.rsqrt(ssq / inner + EPS)
    o_ref[...] = (y_ref[...] * rms * ng_ref[...]).astype(BF16)


def _ssd_scan(lay, zx, xbc, dt, alog, d_skip, norm_g):
    t = zx.shape[0]
    inner = d_skip.shape[0] * SSD_HEAD_DIM
    heads = d_skip.shape[0]
    hpg = heads // SSD_GROUPS
    gn = SSD_GROUPS * SSD_STATE
    assert 2 * heads <= LANES and inner % gn == 0 and hpg % 2 == 0
    xb, bb, cb = 0, inner // gn, inner // gn + 1

    lane_head = jnp.arange(LANES)[:, None]
    col_head = (jnp.arange(inner) // SSD_HEAD_DIM)[None, :]
    exp_f = (lane_head == col_head).astype(BF16)
    exp_b = (lane_head == col_head + heads).astype(BF16)
    dskip = jnp.repeat(d_skip, SSD_HEAD_DIM).reshape(1, inner)

    def specs(reverse):
        cm = lambda b, s: lay.scan_block(b, s, reverse)
        rows = lay.scan_rows
        return dict(
            x=pl.BlockSpec((rows, inner), lambda b, s: (cm(b, s), xb)),
            b=pl.BlockSpec((rows, gn), lambda b, s: (cm(b, s), bb)),
            c=pl.BlockSpec((rows, gn), lambda b, s: (cm(b, s), cb)),
            dt=pl.BlockSpec((rows, LANES), lambda b, s: (cm(b, s), 0)),
            row=pl.BlockSpec((rows, inner), lambda b, s: (cm(b, s), 0)),
        )

    sb = specs(True)
    yb = pl.pallas_call(
        functools.partial(_ssd_bwd_kernel, groups=SSD_GROUPS),
        grid=(lay.nb, lay.scan_steps),
        in_specs=[sb["x"], sb["b"], sb["c"], sb["dt"], _const_spec((1, LANES)), _const_spec((LANES, inner))],
        out_specs=sb["row"],
        out_shape=jax.ShapeDtypeStruct((t, inner), BF16),
        scratch_shapes=[pltpu.VMEM((SSD_STATE, inner), F32)],
        compiler_params=_cparams("arbitrary", "arbitrary"),
        name="ssd_scan_bwd",
    )(xbc, xbc, xbc, dt, alog, exp_b)

    sf = specs(False)
    return pl.pallas_call(
        functools.partial(_ssd_fwd_kernel, groups=SSD_GROUPS, hpg=hpg, p=SSD_HEAD_DIM),
        grid=(lay.nb, lay.scan_steps),
        in_specs=[sf["x"], sf["b"], sf["c"], sf["dt"], sf["row"], sf["row"], _const_spec((1, LANES)),
                  _const_spec((LANES, inner)), _const_spec((LANES, inner)), _const_spec((1, inner)),
                  _const_spec((1, inner))],
        out_specs=sf["row"],
        out_shape=jax.ShapeDtypeStruct((t, inner), BF16),
        scratch_shapes=[pltpu.VMEM((SSD_STATE, inner), F32), pltpu.VMEM((CHUNK, inner), F32)],
        compiler_params=_cparams("arbitrary", "arbitrary"),
        name="ssd_scan_fwd",
    )(xbc, xbc, xbc, dt, zx, yb, alog, exp_f, exp_b, dskip, norm_g.reshape(1, inner))


def _gmlp_gate_kernel(u_ref, v_ref, ng_ref, ws_ref, bias_ref, o_ref, *, groups):
    tm, width = u_ref.shape
    gd = width // groups
    v = v_ref[...].astype(F32)
    ms = jnp.mean(v * v, axis=-1, keepdims=True)
    vn = (v * lax.rsqrt(ms + EPS) * ng_ref[...]).astype(BF16)
    for c in range(tm // CHUNK):
        rs = slice(c * CHUNK, (c + 1) * CHUNK)
        for g in range(groups):
            gs = slice(g * gd, (g + 1) * gd)
            mixed = jnp.dot(ws_ref[g], vn[rs, gs], preferred_element_type=F32) + bias_ref[:, gs]
            o_ref[rs, gs] = (u_ref[rs, gs].astype(F32) * mixed).astype(BF16)


def _gmlp_gate(lay, uv, norm_g, w_s, b_s):
    t = uv.shape[0]
    width = uv.shape[1] // 2
    groups = w_s.shape[0]
    tm = _pick_tile((512, 256, 128), lay.seq, lay.t_ctx)
    bias = jnp.repeat(b_s.T, width // groups, axis=1)
    return pl.pallas_call(
        functools.partial(_gmlp_gate_kernel, groups=groups),
        grid=(t // tm,),
        in_specs=[
            pl.BlockSpec((tm, width), lambda i: (i, 0)),
            pl.BlockSpec((tm, width), lambda i: (i, 1)),
            _const_spec((1, width)),
            _const_spec((groups, CHUNK, CHUNK)),
            _const_spec((CHUNK, width)),
        ],
        out_specs=pl.BlockSpec((tm, width), lambda i: (i, 0)),
        out_shape=jax.ShapeDtypeStruct((t, width), BF16),
        compiler_params=_cparams("arbitrary"),
        name="gmlp_gate",
    )(uv, uv, norm_g.reshape(1, width), w_s.astype(BF16), bias)


def _ret_decay_tiles(dec_ref, row, h, q):
    return -jnp.exp(jnp.full((q, q), dec_ref[row, h], F32))


def _ret_bwd_kernel(dec_ref, q_ref, k_ref, v_ref, yb_ref, s_ref, *, heads):
    @pl.when(pl.program_id(1) == 0)
    def _():
        s_ref[...] = jnp.zeros_like(s_ref)

    q = CHUNK
    dk = q_ref.shape[1] // heads
    dv = v_ref.shape[1] // heads
    pos = lax.broadcasted_iota(jnp.int32, (q, q), 0).astype(F32)
    tiles = []
    for h in range(heads):
        lg = _ret_decay_tiles(dec_ref, 1, h, q)
        tiles.append(dict(
            e_in=jnp.concatenate([jnp.exp(lg * (q - pos))] * (dv // q), axis=1),
            tail=jnp.concatenate([jnp.exp(lg * pos)] * (dk // q), axis=1),
            full=jnp.concatenate([jnp.exp(lg[0:1, :] * q)] * (dv // q), axis=1)))

    def chunk(q_ref, k_ref, v_ref, yb_ref):
        for h in range(heads):
            s_h = s_ref[h]
            ks = slice(h * dk, (h + 1) * dk)
            vs = slice(h * dv, (h + 1) * dv)
            y = jnp.dot(q_ref[:, ks], s_h.astype(BF16), preferred_element_type=F32)
            yb_ref[:, vs] = (y * tiles[h]["e_in"]).astype(BF16)
            kt = (k_ref[:, ks].astype(F32) * tiles[h]["tail"]).astype(BF16)
            upd = lax.dot_general(kt, v_ref[:, vs], (((0,), (0,)), ((), ())), preferred_element_type=F32)
            s_ref[h] = s_h * tiles[h]["full"] + upd

    _for_each_chunk([q_ref, k_ref, v_ref, yb_ref], True, chunk)


def _ret_fwd_kernel(dec_ref, q_ref, k_ref, v_ref, g_ref, yb_ref, o_ref, s_ref, *, heads):
    @pl.when(pl.program_id(1) == 0)
    def _():
        s_ref[...] = jnp.zeros_like(s_ref)

    q = CHUNK
    dk = q_ref.shape[1] // heads
    dv = v_ref.shape[1] // heads
    pos = lax.broadcasted_iota(jnp.int32, (q, q), 0).astype(F32)
    diff = pos - lax.broadcasted_iota(jnp.int32, (q, q), 1).astype(F32)
    tiles = []
    for h in range(heads):
        lg_f = _ret_decay_tiles(dec_ref, 0, h, q)
        lg_b = _ret_decay_tiles(dec_ref, 1, h, q)
        tiles.append(dict(
            dec=(jnp.where(diff >= 0, jnp.exp(lg_f * jnp.maximum(diff, 0.0)), 0.0)
                 + jnp.where(diff <= 0, jnp.exp(lg_b * jnp.maximum(-diff, 0.0)), 0.0)),
            e_in=jnp.concatenate([jnp.exp(lg_f * (pos + 1.0))] * (dv // q), axis=1),
            tail=jnp.concatenate([jnp.exp(lg_f * (q - 1.0 - pos))] * (dk // q), axis=1),
            full=jnp.concatenate([jnp.exp(lg_f[0:1, :] * q)] * (dv // q), axis=1)))

    def chunk(q_ref, k_ref, v_ref, g_ref, yb_ref, o_ref):
        for h in range(heads):
            s_h = s_ref[h]
            ks = slice(h * dk, (h + 1) * dk)
            vs = slice(h * dv, (h + 1) * dv)
            q_h, k_h, v_h = q_ref[:, ks], k_ref[:, ks], v_ref[:, vs]
            scores = lax.dot_general(q_h, k_h, (((1,), (1,)), ((), ())), preferred_element_type=F32)
            y = jnp.dot((scores * tiles[h]["dec"]).astype(BF16), v_h, preferred_element_type=F32)
            y = y + jnp.dot(q_h, s_h.astype(BF16), preferred_element_type=F32) * tiles[h]["e_in"]
            y = y + yb_ref[:, vs].astype(F32)
            y = y * lax.rsqrt(jnp.mean(y * y, axis=-1, keepdims=True) + EPS)
            o_ref[:, vs] = (_silu(g_ref[:, vs].astype(F32)) * y).astype(BF16)
            kt = (k_h.astype(F32) * tiles[h]["tail"]).astype(BF16)
            upd = lax.dot_general(kt, v_h, (((0,), (0,)), ((), ())), preferred_element_type=F32)
            s_ref[h] = s_h * tiles[h]["full"] + upd

    _for_each_chunk([q_ref, k_ref, v_ref, g_ref, yb_ref, o_ref], False, chunk)


def _ret_scan(lay, qkvg, decay_f, decay_b, d):
    t = qkvg.shape[0]
    heads = decay_f.shape[0]
    dv = 2 * d
    dec = jnp.stack([decay_f, decay_b]).astype(F32)
    smem = pl.BlockSpec(memory_space=pltpu.SMEM)
    state = pltpu.VMEM((heads, d // heads, dv // heads), F32)

    def specs(reverse):
        cm = lambda b, s: lay.scan_block(b, s, reverse)
        rows = lay.scan_rows
        return dict(
            q=pl.BlockSpec((rows, d), lambda b, s: (cm(b, s), 0)),
            k=pl.BlockSpec((rows, d), lambda b, s: (cm(b, s), 1)),
            v=pl.BlockSpec((rows, dv), lambda b, s: (cm(b, s), 1)),
            g=pl.BlockSpec((rows, dv), lambda b, s: (cm(b, s), 2)),
            row=pl.BlockSpec((rows, dv), lambda b, s: (cm(b, s), 0)),
        )

    sb = specs(True)
    yb = pl.pallas_call(
        functools.partial(_ret_bwd_kernel, heads=heads),
        grid=(lay.nb, lay.scan_steps),
        in_specs=[smem, sb["q"], sb["k"], sb["v"]],
        out_specs=sb["row"],
        out_shape=jax.ShapeDtypeStruct((t, dv), BF16),
        scratch_shapes=[state],
        compiler_params=_cparams("arbitrary", "arbitrary"),
        name="ret_scan_bwd",
    )(dec, qkvg, qkvg, qkvg)

    sf = specs(False)
    return pl.pallas_call(
        functools.partial(_ret_fwd_kernel, heads=heads),
        grid=(lay.nb, lay.scan_steps),
        in_specs=[smem, sf["q"], sf["k"], sf["v"], sf["g"], sf["row"]],
        out_specs=sf["row"],
        out_shape=jax.ShapeDtypeStruct((t, dv), BF16),
        scratch_shapes=[state],
        compiler_params=_cparams("arbitrary", "arbitrary"),
        name="ret_scan_fwd",
    )(dec, qkvg, qkvg, qkvg, qkvg, yb)


def _ffn_kernel(x_ref, y_ref, g1_ref, sh_ref, sc_ref, g2_ref, ng_ref, wo_ref, w1_ref, w3_ref, w2_ref, fg_ref,
                o_ref, acc_ref, *, n_ff, final_norm):
    x1 = x_ref[...] + g1_ref[...] * jnp.dot(y_ref[...], wo_ref[...], preferred_element_type=F32)
    hb = _norm_mod(x1, ng_ref[...], sh_ref[...], sc_ref[...]).astype(BF16)
    for f in range(n_ff):
        fs = slice(f * FF_CHUNK, (f + 1) * FF_CHUNK)
        a = jnp.dot(hb, w1_ref[:, fs], preferred_element_type=F32)
        b = jnp.dot(hb, w3_ref[:, fs], preferred_element_type=F32)
        part = jnp.dot((_silu(a) * b).astype(BF16), w2_ref[fs, :], preferred_element_type=F32)
        if f == 0:
            acc_ref[...] = part
        else:
            acc_ref[...] += part
    out = x1 + g2_ref[...] * acc_ref[...]
    if final_norm:
        ms = jnp.mean(out * out, axis=-1, keepdims=True)
        out = out * lax.rsqrt(ms + EPS) * fg_ref[...]
    o_ref[...] = out


def _ffn(lay, xall, y, modl, norm2_g, w_out, w1, w3, w2, final_g, *, last):
    d = xall.shape[1]
    dy = y.shape[1]
    dff = w1.shape[1]
    assert dff % FF_CHUNK == 0
    tm = _pick_tile((512, 256, 128), lay.seq, lay.t_ctx)
    t_out = lay.t_lat if last else lay.t_all
    return pl.pallas_call(
        functools.partial(_ffn_kernel, n_ff=dff // FF_CHUNK, final_norm=last),
        grid=(t_out // tm,),
        in_specs=[
            pl.BlockSpec((tm, d), lambda i: (i, 0)),
            pl.BlockSpec((tm, dy), lambda i: (i, 0)),
            _mod_spec(lay, tm, 2, d), _mod_spec(lay, tm, 3, d), _mod_spec(lay, tm, 4, d), _mod_spec(lay, tm, 5, d),
            _const_spec((1, d)),
            _const_spec((dy, d)), _const_spec((d, dff)), _const_spec((d, dff)), _const_spec((dff, d)),
            _const_spec((1, d)),
        ],
        out_specs=pl.BlockSpec((tm, d), lambda i: (i, 0)),
        out_shape=jax.ShapeDtypeStruct((t_out, d), F32),
        scratch_shapes=[pltpu.VMEM((tm, d), F32)],
        compiler_params=_cparams("arbitrary"),
        name="ffn",
    )(xall, y, modl, modl, modl, modl, norm2_g, w_out, w1, w3, w2, final_g)


def kernel(x, c, ctx, c_ctx, mod_w, mod_b, norm1_g, norm2_g, ffn_w1, ffn_w3, ffn_w2, ssd_w_in, ssd_conv_w, ssd_conv_b, ssd_a_log_f, ssd_a_log_b, ssd_dt_bias_f, ssd_dt_bias_b, ssd_d, ssd_norm_g, ssd_w_out, gmlp_w_in, gmlp_norm_g, gmlp_w_s, gmlp_b_s, gmlp_w_out, ret_w_in, ret_decay_f, ret_decay_b, ret_w_out, final_g):
    nb, seq, d = x.shape
    ctx_len = ctx.shape[1]
    depth = mod_w.shape[0]
    assert seq % CHUNK == 0 and ctx_len % CHUNK == 0 and seq % GRID_W == 0
    lay = _Layout(nb, seq, ctx_len)

    rows = -(-(nb + 1) // 8) * 8
    cvec = jnp.zeros((rows, d), F32).at[:nb].set(c).at[nb].set(c_ctx)
    mod = _modulation(cvec, mod_w, mod_b)

    xall = jnp.concatenate([x.reshape(nb * seq, d), ctx.reshape(nb * ctx_len, d)], axis=0)
    final_g2 = final_g.reshape(1, d)

    for i in range(depth):
        last = i == depth - 1
        kind, j = i % N_MIXERS, i // N_MIXERS
        modl = mod[i].reshape(rows * 6, 1, d)
        g1 = norm1_g[i].reshape(1, d)
        if kind == 0:
            heads = ssd_d.shape[1]
            inner = heads * SSD_HEAD_DIM
            n_main = 2 * inner + 2 * SSD_GROUPS * SSD_STATE
            w_in = ssd_w_in[j]
            w_dt = jnp.zeros((d, LANES), F32).at[:, :2 * heads].set(w_in[:, n_main:]).astype(BF16)
            dt_bias = jnp.zeros((1, LANES), F32).at[0, :2 * heads].set(
                jnp.concatenate([ssd_dt_bias_f[j], ssd_dt_bias_b[j]]))
            alog = jnp.zeros((1, LANES), F32).at[0, :2 * heads].set(
                jnp.concatenate([ssd_a_log_f[j], ssd_a_log_b[j]]))
            zx, xbc, dt = _ssd_proj(lay, xall, g1, modl, w_in[:, :inner].astype(BF16),
                                    w_in[:, inner:n_main].astype(BF16), ssd_conv_w[j], ssd_conv_b[j], w_dt, dt_bias)
            y = _ssd_scan(lay, zx, xbc, dt, alog, ssd_d[j], ssd_norm_g[j])
            w_out = ssd_w_out[j]
        elif kind == 1:
            uv = _gmlp_proj(lay, xall, g1, modl, gmlp_w_in[j].astype(BF16))
            y = _gmlp_gate(lay, uv, gmlp_norm_g[j], gmlp_w_s[j], gmlp_b_s[j])
            w_out = gmlp_w_out[j]
        else:
            qkvg = _ret_proj(lay, xall, g1, modl, ret_w_in[j].astype(BF16))
            y = _ret_scan(lay, qkvg, ret_decay_f[j], ret_decay_b[j], d)
            w_out = ret_w_out[j]
        xall = _ffn(lay, xall, y, modl, norm2_g[i].reshape(1, d), w_out.astype(BF16),
                    ffn_w1[i].astype(BF16), ffn_w3[i].astype(BF16), ffn_w2[i].astype(BF16), final_g2, last=last)
    return xall.reshape(nb, seq, d)
```
